```python
import math
import jax
import jax.numpy as jnp
from jax import lax
import numpy as np

D_MODEL = 1024
BATCH = 4
SEQ = 4096
DEPTH = 2

D_FF = 2816
D_RNN = 1024
RNN_BLOCKS = 16
RNN_BLOCK_W = D_RNN // RNN_BLOCKS
RNN_CONV_W = 4
RG_LRU_C = 8.0
D_SCONV = 1024
SCONV_W = 3
N_HEADS = 8
N_KV_HEADS = 2
HEAD_DIM = 128
N_IDX_HEADS = 8
IDX_DIM = 64
INDEX_TOPK_MAX = 256
Q_BLOCK = 128
ROPE_THETA = 500000.0
ROPE_FRACTION_DEN = 4
N_BRANCHES = 3
RMS_EPS = 1e-6

kernel_name = "hybrid_rglru_shortconv_dsa_macaron"

IN_SIZES = (D_RNN, D_RNN,
            D_SCONV, D_SCONV, D_SCONV,
            N_HEADS * HEAD_DIM, N_KV_HEADS * HEAD_DIM, N_KV_HEADS * HEAD_DIM,
            N_IDX_HEADS * IDX_DIM, IDX_DIM, N_IDX_HEADS,
            N_BRANCHES * D_MODEL)
D_IN = sum(IN_SIZES)


def rms_norm(x, g):
    xf = x.astype(jnp.float32)
    y = xf * lax.rsqrt(jnp.mean(xf * xf, axis=-1, keepdims=True) + RMS_EPS)
    return (y * g.astype(jnp.float32)).astype(x.dtype)


def swiglu(x, w_gate_up, w_down):
    g, u = jnp.split(x @ w_gate_up, 2, axis=-1)
    return (jax.nn.silu(g) * u) @ w_down


def causal_dwconv(x, w):
    width = w.shape[0]
    s = x.shape[1]
    xp = jnp.pad(x, ((0, 0), (width - 1, 0), (0, 0)))
    return sum(xp[:, k:k + s] * w[k] for k in range(width))


def partial_rope(x, positions):
    dh = x.shape[-1]
    rot = dh // ROPE_FRACTION_DEN
    half = rot // 2
    inv_freq = ROPE_THETA ** (-jnp.arange(0, rot, 2, dtype=jnp.float32) / rot)
    ang = positions.astype(jnp.float32)[..., None] * inv_freq
    cos = jnp.cos(ang)[:, :, None, :]
    sin = jnp.sin(ang)[:, :, None, :]
    xf = x.astype(jnp.float32)
    x1, x2, xp = xf[..., :half], xf[..., half:rot], xf[..., rot:]
    out = jnp.concatenate([x1 * cos - x2 * sin, x2 * cos + x1 * sin, xp], axis=-1)
    return out.astype(x.dtype)


def rg_lru(x, wa, ba, wx, bx, lam):
    b, s, c = x.shape
    xb = x.reshape(b, s, RNN_BLOCKS, RNN_BLOCK_W)
    r = jax.nn.sigmoid(jnp.einsum('bsni,nij->bsnj', xb, wa).reshape(b, s, c) + ba)
    i = jax.nn.sigmoid(jnp.einsum('bsni,nij->bsnj', xb, wx).reshape(b, s, c) + bx)
    log_a = -RG_LRU_C * r.astype(jnp.float32) * jax.nn.softplus(-lam.astype(jnp.float32))
    a = jnp.exp(log_a)
    in_scale = jnp.sqrt(-jnp.expm1(2.0 * log_a))
    u = in_scale * (i * x).astype(jnp.float32)

    def combine(left, right):
        a1, b1 = left
        a2, b2 = right
        return a1 * a2, a2 * b1 + b2

    _, h = lax.associative_scan(combine, (a, u), axis=1)
    return h.astype(x.dtype)


def dsa_attention(q, k, v, q_idx, k_idx, w_idx):
    b, s, h, dh = q.shape
    g = k.shape[2]
    hpg = h // g
    top_k = min(INDEX_TOPK_MAX, s // 4)
    n_blk = s // Q_BLOCK
    key_pos = jnp.arange(s)
    w_scale = (N_IDX_HEADS ** -0.5) * (IDX_DIM ** -0.5)
    attn_scale = dh ** -0.5

    def to_blocks(t):
        t = t.reshape((b, n_blk, Q_BLOCK) + t.shape[2:])
        return jnp.moveaxis(t, 1, 0)

    k_idx_f = k_idx.astype(jnp.float32)

    def one_block(args):
        qb, qib, wb, start = args
        q_pos = start + jnp.arange(Q_BLOCK)
        rel = jax.nn.relu(jnp.einsum('bqhd,bsd->bqhs', qib.astype(jnp.float32), k_idx_f))
        score = jnp.einsum('bqh,bqhs->bqs', wb.astype(jnp.float32) * w_scale, rel)
        causal = key_pos[None, :] <= q_pos[:, None]
        score = jnp.where(causal[None], score, -jnp.inf)
        _, idx = lax.top_k(score, top_k)
        sel_ok = idx <= q_pos[None, :, None]
        kg = jax.vmap(lambda kk, ii: kk[ii])(k, idx)
        vg = jax.vmap(lambda vv, ii: vv[ii])(v, idx)
        qg = qb.reshape(b, Q_BLOCK, g, hpg, dh)
        logits = jnp.einsum('bqgjd,bqkgd->bqgjk', qg, kg).astype(jnp.float32) * attn_scale
        logits = jnp.where(sel_ok[:, :, None, None, :], logits, -jnp.inf)
        p = jax.nn.softmax(logits, axis=-1).astype(vg.dtype)
        o = jnp.einsum('bqgjk,bqkgd->bqgjd', p, vg)
        return o.reshape(b, Q_BLOCK, h * dh)

    starts = jnp.arange(n_blk) * Q_BLOCK
    out = lax.map(one_block, (to_blocks(q), to_blocks(q_idx), to_blocks(w_idx), starts))
    return jnp.moveaxis(out, 0, 1).reshape(b, s, h * dh)


def token_mixing(u, positions, w_in, rnn_conv_w, rnn_conv_b, rnn_gate_a_w, rnn_gate_a_b,
                 rnn_gate_x_w, rnn_gate_x_b, rnn_lambda, rnn_w_out, sconv_w, sconv_w_out,
                 attn_w_out, w_o):
    b, s, _ = u.shape
    proj = u @ w_in
    cuts = []
    acc = 0
    for sz in IN_SIZES[:-1]:
        acc += sz
        cuts.append(acc)
    (rnn_x, rnn_g, c_b, c_c, c_h, q, k, v, qi, ki, wi, gates) = jnp.split(proj, cuts, axis=-1)

    xa = causal_dwconv(rnn_x, rnn_conv_w) + rnn_conv_b
    ha = rg_lru(xa, rnn_gate_a_w, rnn_gate_a_b, rnn_gate_x_w, rnn_gate_x_b, rnn_lambda)
    ya = (ha * jax.nn.gelu(rnn_g)) @ rnn_w_out

    yb = (c_b * causal_dwconv(c_c * c_h, sconv_w)) @ sconv_w_out

    q = partial_rope(q.reshape(b, s, N_HEADS, HEAD_DIM), positions)
    k = partial_rope(k.reshape(b, s, N_KV_HEADS, HEAD_DIM), positions)
    v = v.reshape(b, s, N_KV_HEADS, HEAD_DIM)
    qi = partial_rope(qi.reshape(b, s, N_IDX_HEADS, IDX_DIM), positions)
    ki = partial_rope(ki.reshape(b, s, 1, IDX_DIM), positions)[:, :, 0]
    yc = dsa_attention(q, k, v, qi, ki, wi) @ attn_w_out

    gt = jax.nn.sigmoid(gates.reshape(b, s, N_BRANCHES, D_MODEL))
    merged = gt[:, :, 0] * ya + gt[:, :, 1] * yb + gt[:, :, 2] * yc
    return merged @ w_o


def setup_inputs(seed: int = 0) -> dict:
    key = jax.random.key(seed)
    ks = iter(jax.random.split(key, 32))
    f32 = jnp.float32

    def nrm(shape, fan_in):
        return jax.random.normal(next(ks), shape, f32) * (fan_in ** -0.5)

    def gain(shape):
        return 1.0 + 0.02 * jax.random.normal(next(ks), shape, f32)

    def bias(shape):
        return 0.01 * jax.random.normal(next(ks), shape, f32)

    L = DEPTH
    x = jax.random.normal(next(ks), (BATCH, SEQ, D_MODEL), f32)
    offs = jax.random.randint(next(ks), (BATCH, 1), 0, 1024, dtype=jnp.int32)
    positions = offs + jnp.arange(SEQ, dtype=jnp.int32)[None, :]
    a0 = jax.random.uniform(next(ks), (L, D_RNN), f32, 0.9, 0.999)
    sig = a0 ** (1.0 / RG_LRU_C)
    rnn_lambda = jnp.log(sig) - jnp.log1p(-sig)
    return {
        "x": x,
        "positions": positions,
        "ffn1_norm": gain((L, D_MODEL)),
        "ffn1_w_gate_up": nrm((L, D_MODEL, 2 * D_FF), D_MODEL),
        "ffn1_w_down": nrm((L, D_FF, D_MODEL), D_FF),
        "mix_norm": gain((L, D_MODEL)),
        "w_in": nrm((L, D_MODEL, D_IN), D_MODEL),
        "rnn_conv_w": nrm((L, RNN_CONV_W, D_RNN), RNN_CONV_W),
        "rnn_conv_b": bias((L, D_RNN)),
        "rnn_gate_a_w": nrm((L, RNN_BLOCKS, RNN_BLOCK_W, RNN_BLOCK_W), RNN_BLOCK_W),
        "rnn_gate_a_b": bias((L, D_RNN)),
        "rnn_gate_x_w": nrm((L, RNN_BLOCKS, RNN_BLOCK_W, RNN_BLOCK_W), RNN_BLOCK_W),
        "rnn_gate_x_b": bias((L, D_RNN)),
        "rnn_lambda": rnn_lambda,
        "rnn_w_out": nrm((L, D_RNN, D_MODEL), D_RNN),
        "sconv_w": nrm((L, SCONV_W, D_SCONV), SCONV_W),
        "sconv_w_out": nrm((L, D_SCONV, D_MODEL), D_SCONV),
        "attn_w_out": nrm((L, N_HEADS * HEAD_DIM, D_MODEL), N_HEADS * HEAD_DIM),
        "w_o": nrm((L, D_MODEL, D_MODEL), D_MODEL),
        "ffn2_norm": gain((L, D_MODEL)),
        "ffn2_w_gate_up": nrm((L, D_MODEL, 2 * D_FF), D_MODEL),
        "ffn2_w_down": nrm((L, D_FF, D_MODEL), D_FF),
        "final_norm": gain((D_MODEL,)),
    }


def reference(x, positions, ffn1_norm, ffn1_w_gate_up, ffn1_w_down, mix_norm, w_in,
              rnn_conv_w, rnn_conv_b, rnn_gate_a_w, rnn_gate_a_b, rnn_gate_x_w, rnn_gate_x_b,
              rnn_lambda, rnn_w_out, sconv_w, sconv_w_out, attn_w_out, w_o,
              ffn2_norm, ffn2_w_gate_up, ffn2_w_down, final_norm):
    for l in range(DEPTH):
        x = x + 0.5 * swiglu(rms_norm(x, ffn1_norm[l]), ffn1_w_gate_up[l], ffn1_w_down[l])
        x = x + token_mixing(rms_norm(x, mix_norm[l]), positions, w_in[l],
                             rnn_conv_w[l], rnn_conv_b[l], rnn_gate_a_w[l], rnn_gate_a_b[l],
                             rnn_gate_x_w[l], rnn_gate_x_b[l], rnn_lambda[l], rnn_w_out[l],
                             sconv_w[l], sconv_w_out[l], attn_w_out[l], w_o[l])
        x = x + 0.5 * swiglu(rms_norm(x, ffn2_norm[l]), ffn2_w_gate_up[l], ffn2_w_down[l])
    return rms_norm(x, final_norm)
```

```python
import functools
import math

import jax
import jax.numpy as jnp
import numpy as np
from jax import lax
from jax.experimental import pallas as pl
from jax.experimental.pallas import tpu as pltpu

F32 = jnp.float32
BF16 = jnp.bfloat16

D_MODEL = 1024
D_FF = 2816
D_RNN = 1024
RNN_BLOCKS = 16
RNN_BLOCK_W = D_RNN // RNN_BLOCKS
RNN_CONV_W = 4
RG_LRU_C = 8.0
D_SCONV = 1024
SCONV_W = 3
N_HEADS = 8
N_KV_HEADS = 2
HEAD_DIM = 128
N_IDX_HEADS = 8
IDX_DIM = 64
INDEX_TOPK_MAX = 256
ROPE_THETA = 500000.0
ROPE_FRACTION_DEN = 4
N_BRANCHES = 3
RMS_EPS = 1e-6

LANES = 128
SUBLANES = 8
VMEM_LIMIT = 56 * 1024 * 1024

FFN_TM = 1024
FFN_TF = 256
RNN_TM = 256
SCONV_TM = 512
QKV_TM = 512
MERGE_TM = 512
TAB_TM = 1024
ATT_TQ = 128
ATT_TK = 512

GATE_GROUP = 256
N_GATE_GROUPS = D_RNN // GATE_GROUP

QKV_COLS = N_HEADS * HEAD_DIM + 2 * N_KV_HEADS * HEAD_DIM + N_IDX_HEADS * IDX_DIM
KIWI_COLS = LANES
NEG_BIG = -1e30
KEY_NEG_INF = np.int32(np.uint32(0x807FFFFF).view(np.int32))


def _cparams(sem):
    return pltpu.CompilerParams(dimension_semantics=sem, vmem_limit_bytes=VMEM_LIMIT)


def _rmsnorm_f32(x, g):
    var = jnp.mean(x * x, axis=-1, keepdims=True)
    return (x * lax.rsqrt(var + RMS_EPS)) * g


def _bdot(a, b):
    return jnp.dot(a, b, preferred_element_type=F32)


def _dot_nt(a, b):
    return lax.dot_general(a, b, (((1,), (1,)), ((), ())), preferred_element_type=F32)


def _ffn_body(x_ref, g_ref, wg_ref, wu_ref, wd_ref, fn_ref, o_ref, xn_ref, acc_ref, *, n_ff, final_norm):
    j = pl.program_id(1)

    @pl.when(j == 0)
    def _():
        xn_ref[...] = _rmsnorm_f32(x_ref[...], g_ref[...]).astype(BF16)
        acc_ref[...] = jnp.zeros_like(acc_ref)

    xn = xn_ref[...]
    g = _bdot(xn, wg_ref[...])
    u = _bdot(xn, wu_ref[...])
    a = (g * jax.nn.sigmoid(g) * u).astype(BF16)
    acc_ref[...] += _bdot(a, wd_ref[...])

    @pl.when(j == n_ff - 1)
    def _():
        y = x_ref[...] + 0.5 * acc_ref[...]
        if final_norm:
            y = _rmsnorm_f32(y, fn_ref[...])
        o_ref[...] = y


def _ffn(x, gain, w_gu, w_d, fnorm, final_norm):
    n = x.shape[0]
    n_ff = D_FF // FFN_TF
    grid = (n // FFN_TM, n_ff)
    return pl.pallas_call(
        functools.partial(_ffn_body, n_ff=n_ff, final_norm=final_norm),
        out_shape=jax.ShapeDtypeStruct((n, D_MODEL), F32),
        grid=grid,
        in_specs=[
            pl.BlockSpec((FFN_TM, D_MODEL), lambda i, j: (i, 0)),
            pl.BlockSpec((1, D_MODEL), lambda i, j: (0, 0)),
            pl.BlockSpec((D_MODEL, FFN_TF), lambda i, j: (0, j)),
            pl.BlockSpec((D_MODEL, FFN_TF), lambda i, j: (0, j + n_ff)),
            pl.BlockSpec((FFN_TF, D_MODEL), lambda i, j: (j, 0)),
            pl.BlockSpec((1, D_MODEL), lambda i, j: (0, 0)),
        ],
        out_specs=pl.BlockSpec((FFN_TM, D_MODEL), lambda i, j: (i, 0)),
        scratch_shapes=[pltpu.VMEM((FFN_TM, D_MODEL), BF16), pltpu.VMEM((FFN_TM, D_MODEL), F32)],
        compiler_params=_cparams(("arbitrary", "arbitrary")),
        name="ffn",
    )(x, gain, w_gu, w_gu, w_d, fnorm)


def _shift_rows(x, d, fill):
    tm = x.shape[0]
    if d % SUBLANES == 0:
        head = jnp.full((d, x.shape[1]), fill, x.dtype)
        return jnp.concatenate([head, x[: tm - d]], axis=0)
    row = lax.broadcasted_iota(jnp.int32, x.shape, 0)
    return jnp.where(row >= d, pltpu.roll(x, d, 0), fill)


def _rnn_body(x_ref, g_ref, w_ref, cw_ref, cb_ref, wbd_ref, ba_ref, bx_ref, lam_ref, wout_ref,
              o_ref, xbuf_ref, h_ref, *, tiles_per_seq):
    tm = x_ref.shape[0]
    i = pl.program_id(0)

    @pl.when(i % tiles_per_seq == 0)
    def _():
        xbuf_ref[0:SUBLANES, :] = jnp.zeros((SUBLANES, D_RNN), F32)
        h_ref[...] = jnp.zeros_like(h_ref)

    u = _rmsnorm_f32(x_ref[...], g_ref[...]).astype(BF16)
    p = _bdot(u, w_ref[...])
    rx = p[:, :D_RNN]
    rg = p[:, D_RNN:]

    xbuf_ref[SUBLANES:, :] = rx
    xa = cb_ref[...] + rx * cw_ref[RNN_CONV_W - 1:RNN_CONV_W, :]
    for k in range(RNN_CONV_W - 1):
        off = SUBLANES - (RNN_CONV_W - 1) + k
        xa = xa + xbuf_ref[off:off + tm, :] * cw_ref[k:k + 1, :]
    xbuf_ref[0:SUBLANES, :] = rx[tm - SUBLANES:, :]

    xab = xa.astype(BF16)
    r_parts, i_parts = [], []
    for jg in range(N_GATE_GROUPS):
        gj = _bdot(xab[:, jg * GATE_GROUP:(jg + 1) * GATE_GROUP], wbd_ref[jg])
        r_parts.append(gj[:, :GATE_GROUP])
        i_parts.append(gj[:, GATE_GROUP:])
    r = jax.nn.sigmoid(jnp.concatenate(r_parts, axis=1) + ba_ref[...])
    ig = jax.nn.sigmoid(jnp.concatenate(i_parts, axis=1) + bx_ref[...])

    nl = -lam_ref[...]
    sp = jnp.maximum(nl, 0.0) + jnp.log1p(jnp.exp(-jnp.abs(nl)))
    log_a = (-RG_LRU_C) * r * sp
    a = jnp.exp(log_a)
    in_scale = jnp.sqrt(-jnp.tanh(log_a) * (a * a + 1.0))
    b = in_scale * (ig * xa)

    d = 1
    while d < tm:
        a_s = _shift_rows(a, d, 1.0)
        b_s = _shift_rows(b, d, 0.0)
        b = a * b_s + b
        a = a * a_s
        d *= 2
    h = b + a * h_ref[...]
    h_ref[...] = h[tm - 1:tm, :]

    gl = 0.5 * rg * (1.0 + jnp.tanh(math.sqrt(2.0 / math.pi) * (rg + 0.044715 * (rg * rg * rg))))
    o_ref[...] = _bdot((h * gl).astype(BF16), wout_ref[...])


def _rnn_branch(x, gain, w_rnn, conv_w, conv_b, wbd, ba, bx, lam, w_out, seq):
    n = x.shape[0]
    tm = RNN_TM
    full = lambda shape: pl.BlockSpec(shape, lambda i: (0,) * len(shape))
    return pl.pallas_call(
        functools.partial(_rnn_body, tiles_per_seq=seq // tm),
        out_shape=jax.ShapeDtypeStruct((n, D_MODEL), F32),
        grid=(n // tm,),
        in_specs=[
            pl.BlockSpec((tm, D_MODEL), lambda i: (i, 0)),
            full((1, D_MODEL)),
            full((D_MODEL, 2 * D_RNN)),
            full((RNN_CONV_W, D_RNN)),
            full((1, D_RNN)),
            full((N_GATE_GROUPS, GATE_GROUP, 2 * GATE_GROUP)),
            full((1, D_RNN)),
            full((1, D_RNN)),
            full((1, D_RNN)),
            full((D_RNN, D_MODEL)),
        ],
        out_specs=pl.BlockSpec((tm, D_MODEL), lambda i: (i, 0)),
        scratch_shapes=[pltpu.VMEM((tm + SUBLANES, D_RNN), F32), pltpu.VMEM((1, D_RNN), F32)],
        compiler_params=_cparams(("arbitrary",)),
        name="rnn_branch",
    )(x, gain, w_rnn, conv_w, conv_b, wbd, ba, bx, lam, w_out)


def _sconv_body(x_ref, g_ref, w_ref, sw_ref, wout_ref, o_ref, zbuf_ref, *, tiles_per_seq):
    tm = x_ref.shape[0]
    i = pl.program_id(0)

    @pl.when(i % tiles_per_seq == 0)
    def _():
        zbuf_ref[0:SUBLANES, :] = jnp.zeros((SUBLANES, D_SCONV), F32)

    u = _rmsnorm_f32(x_ref[...], g_ref[...]).astype(BF16)
    p = _bdot(u, w_ref[...])
    c_b = p[:, :D_SCONV]
    z = p[:, D_SCONV:2 * D_SCONV] * p[:, 2 * D_SCONV:]
    zbuf_ref[SUBLANES:, :] = z
    cv = z * sw_ref[SCONV_W - 1:SCONV_W, :]
    for k in range(SCONV_W - 1):
        off = SUBLANES - (SCONV_W - 1) + k
        cv = cv + zbuf_ref[off:off + tm, :] * sw_ref[k:k + 1, :]
    zbuf_ref[0:SUBLANES, :] = z[tm - SUBLANES:, :]
    o_ref[...] = _bdot((c_b * cv).astype(BF16), wout_ref[...])


def _sconv_branch(x, gain, w_bch, sw, w_out, seq):
    n = x.shape[0]
    tm = SCONV_TM
    full = lambda shape: pl.BlockSpec(shape, lambda i: (0,) * len(shape))
    return pl.pallas_call(
        functools.partial(_sconv_body, tiles_per_seq=seq // tm),
        out_shape=jax.ShapeDtypeStruct((n, D_MODEL), F32),
        grid=(n // tm,),
        in_specs=[
            pl.BlockSpec((tm, D_MODEL), lambda i: (i, 0)),
            full((1, D_MODEL)),
            full((D_MODEL, 3 * D_SCONV)),
            full((SCONV_W, D_SCONV)),
            full((D_SCONV, D_MODEL)),
        ],
        out_specs=pl.BlockSpec((tm, D_MODEL), lambda i: (i, 0)),
        scratch_shapes=[pltpu.VMEM((tm + SUBLANES, D_SCONV), F32)],
        compiler_params=_cparams(("arbitrary",)),
        name="sconv_branch",
    )(x, gain, w_bch, sw, w_out)


def _tab_body(pos_ref, f128_ref, s128_ref, f64_ref, s64_ref, c128_ref, sg128_ref, c64_ref, sg64_ref):
    posf = pos_ref[...].astype(F32)
    for f_ref, s_ref, c_out, s_out in ((f128_ref, s128_ref, c128_ref, sg128_ref),
                                       (f64_ref, s64_ref, c64_ref, sg64_ref)):
        sgn = s_ref[...]
        ang = posf * f_ref[...]
        c_out[...] = jnp.where(sgn != 0.0, jnp.cos(ang), 1.0)
        s_out[...] = jnp.sin(ang) * sgn


def _rope_lane_rows(period, rot):
    half = rot // 2
    inv_freq = ROPE_THETA ** (-jnp.arange(0, rot, 2, dtype=F32) / rot)
    lane = np.arange(LANES) % period
    f_row = jnp.where(lane < rot, inv_freq[lane % half], 0.0).astype(F32)
    s_row = np.where(lane < half, -1.0, np.where(lane < rot, 1.0, 0.0)).astype(np.float32)
    return f_row.reshape(1, LANES), jnp.asarray(s_row).reshape(1, LANES)


def _rope_tables(pos_col):
    n = pos_col.shape[0]
    f128, s128 = _rope_lane_rows(HEAD_DIM, HEAD_DIM // ROPE_FRACTION_DEN)
    f64, s64 = _rope_lane_rows(IDX_DIM, IDX_DIM // ROPE_FRACTION_DEN)
    row = pl.BlockSpec((1, LANES), lambda i: (0, 0))
    tab = pl.BlockSpec((TAB_TM, LANES), lambda i: (i, 0))
    return pl.pallas_call(
        _tab_body,
        out_shape=[jax.ShapeDtypeStruct((n, LANES), F32)] * 4,
        grid=(n // TAB_TM,),
        in_specs=[pl.BlockSpec((TAB_TM, 1), lambda i: (i, 0)), row, row, row, row],
        out_specs=[tab] * 4,
        compiler_params=_cparams(("arbitrary",)),
        name="rope_tables",
    )(pos_col, f128, s128, f64, s64)


def _rope_apply(xh, c, sg, first_half, half):
    partner = jnp.where(first_half, pltpu.roll(xh, LANES - half, 1), pltpu.roll(xh, half, 1))
    return xh * c + partner * sg


def _qkv_body(x_ref, g_ref, w_ref, c128_ref, s128_ref, c64_ref, s64_ref,
              q_ref, k_ref, v_ref, qi_ref, ki_ref, wi_ref):
    tm = x_ref.shape[0]
    u = _rmsnorm_f32(x_ref[...], g_ref[...]).astype(BF16)
    p = _bdot(u, w_ref[...])
    lane = lax.broadcasted_iota(jnp.int32, (tm, LANES), 1)
    h128 = HEAD_DIM // ROPE_FRACTION_DEN // 2
    h64 = IDX_DIM // ROPE_FRACTION_DEN // 2
    first128 = lane < h128
    first64 = (lane % IDX_DIM) < h64
    c128, s128 = c128_ref[...], s128_ref[...]
    c64, s64 = c64_ref[...], s64_ref[...]

    off = 0
    for h in range(N_HEADS):
        blk = p[:, off + h * LANES: off + (h + 1) * LANES]
        q_ref[:, h * LANES:(h + 1) * LANES] = _rope_apply(blk, c128, s128, first128, h128).astype(BF16)
    off += N_HEADS * HEAD_DIM
    for h in range(N_KV_HEADS):
        blk = p[:, off + h * LANES: off + (h + 1) * LANES]
        k_ref[:, h * LANES:(h + 1) * LANES] = _rope_apply(blk, c128, s128, first128, h128).astype(BF16)
    off += N_KV_HEADS * HEAD_DIM
    v_ref[...] = p[:, off: off + N_KV_HEADS * HEAD_DIM].astype(BF16)
    off += N_KV_HEADS * HEAD_DIM
    for h in range(N_IDX_HEADS * IDX_DIM // LANES):
        blk = p[:, off + h * LANES: off + (h + 1) * LANES]
        qi_ref[:, h * LANES:(h + 1) * LANES] = _rope_apply(blk, c64, s64, first64, h64).astype(BF16)
    off += N_IDX_HEADS * IDX_DIM
    blk = p[:, off: off + LANES]
    ki_ref[...] = _rope_apply(blk, c64, s64, first64, h64)[:, :IDX_DIM].astype(BF16)
    wi_ref[...] = blk[:, IDX_DIM:IDX_DIM + N_IDX_HEADS]


def _qkv_proj(x, gain, w_qkv, tabs):
    n = x.shape[0]
    tm = QKV_TM
    full = lambda shape: pl.BlockSpec(shape, lambda i: (0,) * len(shape))
    tile = lambda w: pl.BlockSpec((tm, w), lambda i: (i, 0))
    kvw = N_KV_HEADS * HEAD_DIM
    return pl.pallas_call(
        _qkv_body,
        out_shape=[
            jax.ShapeDtypeStruct((n, N_HEADS * HEAD_DIM), BF16),
            jax.ShapeDtypeStruct((n, kvw), BF16),
            jax.ShapeDtypeStruct((n, kvw), BF16),
            jax.ShapeDtypeStruct((n, N_IDX_HEADS * IDX_DIM), BF16),
            jax.ShapeDtypeStruct((n, IDX_DIM), BF16),
            jax.ShapeDtypeStruct((n, N_IDX_HEADS), F32),
        ],
        grid=(n // tm,),
        in_specs=[tile(D_MODEL), full((1, D_MODEL)), full((D_MODEL, QKV_COLS + KIWI_COLS)),
                  tile(LANES), tile(LANES), tile(LANES), tile(LANES)],
        out_specs=[tile(N_HEADS * HEAD_DIM), tile(kvw), tile(kvw), tile(N_IDX_HEADS * IDX_DIM),
                   tile(IDX_DIM), tile(N_IDX_HEADS)],
        compiler_params=_cparams(("arbitrary",)),
        name="qkv_proj",
    )(x, gain, w_qkv, *tabs)


def _key_to_float(u):
    key = u + KEY_NEG_INF
    bits = jnp.where(key < 0, key ^ jnp.int32(0x7FFFFFFF), key)
    return lax.bitcast_convert_type(bits, F32)


def _attn_body(q_ref, qi_ref, wi_ref, ki_ref, k_ref, v_ref, o_ref,
               sc_ref, m_ref, l_ref, acc_ref, cut_ref, *, seq, top_k):
    tq, tk = ATT_TQ, ATT_TK
    j = pl.program_id(1)
    q0 = j * tq
    nch = (q0 + tq - 1) // tk + 1
    tpos = q0 + lax.broadcasted_iota(jnp.int32, (tq, 1), 0)
    w_scale = (N_IDX_HEADS ** -0.5) * (IDX_DIM ** -0.5)
    attn_scale = HEAD_DIM ** -0.5

    def key_pos(c):
        return c * tk + lax.broadcasted_iota(jnp.int32, (tq, tk), 1)

    wsc = wi_ref[...] * w_scale

    def score_chunk(c, carry):
        kic = ki_ref[pl.ds(pl.multiple_of(c * tk, tk), tk), :]
        acc = jnp.zeros((tq, tk), F32)
        for h in range(N_IDX_HEADS):
            rel = _dot_nt(qi_ref[:, h * IDX_DIM:(h + 1) * IDX_DIM], kic)
            acc = acc + wsc[:, h:h + 1] * jnp.maximum(rel, 0.0)
        sc_ref[c] = jnp.where(key_pos(c) <= tpos, acc, -jnp.inf)
        return carry

    lax.fori_loop(0, nch, score_chunk, 0)

    def count(pred):
        def body(c, acc):
            ind = jnp.where(pred(sc_ref[c], c), 1.0, 0.0)
            part = ind[:, 0:LANES]
            for t in range(1, tk // LANES):
                part = part + ind[:, t * LANES:(t + 1) * LANES]
            return acc + part
        acc = lax.fori_loop(0, nch, body, jnp.zeros((tq, LANES), F32))
        return jnp.sum(acc, axis=1, keepdims=True)

    kf = float(top_k)

    def radix_step(it, u):
        cand = u | lax.shift_left(jnp.int32(1), 31 - it)
        cf = _key_to_float(cand)
        cnt = count(lambda s, c: s >= cf)
        return jnp.where(cnt >= kf, cand, u)

    u_thr = lax.fori_loop(0, 32, radix_step, jnp.zeros((tq, 1), jnp.int32))
    thr = _key_to_float(u_thr)

    cnt_gt = count(lambda s, c: s > thr)
    cnt_ge = count(lambda s, c: s >= thr)
    need = kf - cnt_gt
    tie = (cnt_ge > kf) & (thr > -jnp.inf)
    cut_ref[...] = jnp.full(cut_ref.shape, seq, jnp.int32)

    @pl.when(jnp.sum(jnp.where(tie, 1.0, 0.0)) > 0.0)
    def _():
        def idx_step(it, v):
            cand = v | lax.shift_left(jnp.int32(1), (seq.bit_length() - 2) - it)
            cnt = count(lambda s, c: (s == thr) & (key_pos(c) < cand))
            return jnp.where(cnt < need, cand, v)
        v = lax.fori_loop(0, seq.bit_length() - 1, idx_step, jnp.zeros((tq, 1), jnp.int32))
        cut_ref[...] = jnp.broadcast_to(jnp.where(tie, v, seq), cut_ref.shape)

    cut = cut_ref[:, 0:1]

    m_ref[...] = jnp.full(m_ref.shape, NEG_BIG, F32)
    l_ref[...] = jnp.zeros_like(l_ref)
    acc_ref[...] = jnp.zeros_like(acc_ref)
    hpg = N_HEADS // N_KV_HEADS

    def attn_chunk(c, carry):
        s = sc_ref[c]
        kp = key_pos(c)
        sel = (kp <= tpos) & ((s > thr) | ((s == thr) & (kp <= cut)))
        r0 = pl.multiple_of(c * tk, tk)
        for g in range(N_KV_HEADS):
            kc = k_ref[pl.ds(r0, tk), g * HEAD_DIM:(g + 1) * HEAD_DIM]
            vc = v_ref[pl.ds(r0, tk), g * HEAD_DIM:(g + 1) * HEAD_DIM]
            for hh in range(hpg):
                h = g * hpg + hh
                lg = _dot_nt(q_ref[:, h * HEAD_DIM:(h + 1) * HEAD_DIM], kc) * attn_scale
                lg = jnp.where(sel, lg, NEG_BIG)
                m_prev = m_ref[h]
                m_new = jnp.maximum(m_prev, jnp.max(lg, axis=1, keepdims=True))
                alpha = jnp.exp(m_prev - m_new)
                pr = jnp.exp(lg - m_new[:, 0:1])
                l_ref[h] = alpha * l_ref[h] + jnp.sum(pr, axis=1, keepdims=True)
                acc_ref[h] = alpha * acc_ref[h] + _bdot(pr.astype(BF16), vc)
                m_ref[h] = m_new
        return carry

    lax.fori_loop(0, nch, attn_chunk, 0)

    for h in range(N_HEADS):
        o_ref[:, h * HEAD_DIM:(h + 1) * HEAD_DIM] = (acc_ref[h] / l_ref[h]).astype(BF16)


def _attention(q, qi, wi, ki, k, v, batch, seq):
    n = q.shape[0]
    tq, tk = ATT_TQ, ATT_TK
    nq = seq // tq
    top_k = min(INDEX_TOPK_MAX, seq // 4)
    kvw = N_KV_HEADS * HEAD_DIM
    qtile = lambda w: pl.BlockSpec((tq, w), lambda b, j: (b * nq + j, 0))
    whole = lambda w: pl.BlockSpec((seq, w), lambda b, j: (b, 0))
    return pl.pallas_call(
        functools.partial(_attn_body, seq=seq, top_k=top_k),
        out_shape=jax.ShapeDtypeStruct((n, N_HEADS * HEAD_DIM), BF16),
        grid=(batch, nq),
        in_specs=[qtile(N_HEADS * HEAD_DIM), qtile(N_IDX_HEADS * IDX_DIM), qtile(N_IDX_HEADS),
                  whole(IDX_DIM), whole(kvw), whole(kvw)],
        out_specs=qtile(N_HEADS * HEAD_DIM),
        scratch_shapes=[
            pltpu.VMEM((seq // tk, tq, tk), F32),
            pltpu.VMEM((N_HEADS, tq, LANES), F32),
            pltpu.VMEM((N_HEADS, tq, LANES), F32),
            pltpu.VMEM((N_HEADS, tq, HEAD_DIM), F32),
            pltpu.VMEM((tq, LANES), jnp.int32),
        ],
        compiler_params=_cparams(("arbitrary", "arbitrary")),
        name="dsa_attention",
    )(q, qi, wi, ki, k, v)


def _merge_body(x_ref, g_ref, wg_ref, ya_ref, yb_ref, oc_ref, wao_ref, wo_ref, o_ref):
    x = x_ref[...]
    u = _rmsnorm_f32(x, g_ref[...]).astype(BF16)
    gt = jax.nn.sigmoid(_bdot(u, wg_ref[...]))
    yc = _bdot(oc_ref[...], wao_ref[...])
    merged = (gt[:, :D_MODEL] * ya_ref[...] + gt[:, D_MODEL:2 * D_MODEL] * yb_ref[...]
              + gt[:, 2 * D_MODEL:] * yc)
    o_ref[...] = x + _bdot(merged.astype(BF16), wo_ref[...])


def _merge(x, gain, w_gates, ya, yb, oc, w_attn_out, w_o):
    n = x.shape[0]
    tm = MERGE_TM
    full = lambda shape: pl.BlockSpec(shape, lambda i: (0,) * len(shape))
    tile = pl.BlockSpec((tm, D_MODEL), lambda i: (i, 0))
    return pl.pallas_call(
        _merge_body,
        out_shape=jax.ShapeDtypeStruct((n, D_MODEL), F32),
        grid=(n // tm,),
        in_specs=[tile, full((1, D_MODEL)), full((D_MODEL, N_BRANCHES * D_MODEL)), tile, tile, tile,
                  full((N_HEADS * HEAD_DIM, D_MODEL)), full((D_MODEL, D_MODEL))],
        out_specs=tile,
        compiler_params=_cparams(("arbitrary",)),
        name="merge",
    )(x, gain, w_gates, ya, yb, oc, w_attn_out, w_o)


def _block_diag_groups(w):
    per = GATE_GROUP // RNN_BLOCK_W
    w4 = w.reshape(N_GATE_GROUPS, per, RNN_BLOCK_W, RNN_BLOCK_W)
    eye = jnp.eye(per, dtype=w.dtype)
    return jnp.einsum("gnij,nm->gnimj", w4, eye).reshape(N_GATE_GROUPS, GATE_GROUP, GATE_GROUP)


def kernel(x, positions, ffn1_norm, ffn1_w_gate_up, ffn1_w_down, mix_norm, w_in, rnn_conv_w, rnn_conv_b,
           rnn_gate_a_w, rnn_gate_a_b, rnn_gate_x_w, rnn_gate_x_b, rnn_lambda, rnn_w_out, sconv_w,
           sconv_w_out, attn_w_out, w_o, ffn2_norm, ffn2_w_gate_up, ffn2_w_down, final_norm):
    batch, seq, d = x.shape
    depth = w_in.shape[0]
    n = batch * seq
    xf = x.reshape(n, d)
    row = lambda v: v.reshape(1, -1)

    tabs = _rope_tables(positions.reshape(n, 1))

    o_rnn = 0
    o_bch = o_rnn + 2 * D_RNN
    o_q = o_bch + 3 * D_SCONV
    o_gates = o_q + QKV_COLS + IDX_DIM + N_IDX_HEADS
    pad = KIWI_COLS - IDX_DIM - N_IDX_HEADS

    for l in range(depth):
        win = w_in[l]
        w_rnn = win[:, o_rnn:o_bch].astype(BF16)
        w_bch = win[:, o_bch:o_q].astype(BF16)
        w_qkv = jnp.pad(win[:, o_q:o_gates], ((0, 0), (0, pad))).astype(BF16)
        w_gates = win[:, o_gates:].astype(BF16)
        wbd = jnp.concatenate([_block_diag_groups(rnn_gate_a_w[l]), _block_diag_groups(rnn_gate_x_w[l])],
                              axis=-1).astype(BF16)

        xf = _ffn(xf, row(ffn1_norm[l]), ffn1_w_gate_up[l].astype(BF16), ffn1_w_down[l].astype(BF16),
                  row(final_norm), False)

        gain = row(mix_norm[l])
        ya = _rnn_branch(xf, gain, w_rnn, rnn_conv_w[l], row(rnn_conv_b[l]), wbd, row(rnn_gate_a_b[l]),
                         row(rnn_gate_x_b[l]), row(rnn_lambda[l]), rnn_w_out[l].astype(BF16), seq)
        yb = _sconv_branch(xf, gain, w_bch, sconv_w[l], sconv_w_out[l].astype(BF16), seq)
        q, k, v, qi, ki, wi = _qkv_proj(xf, gain, w_qkv, tabs)
        oc = _attention(q, qi, wi, ki, k, v, batch, seq)
        xf = _merge(xf, gain, w_gates, ya, yb, oc, attn_w_out[l].astype(BF16), w_o[l].astype(BF16))

        xf = _ffn(xf, row(ffn2_norm[l]), ffn2_w_gate_up[l].astype(BF16), ffn2_w_down[l].astype(BF16),
                  row(final_norm), l == depth - 1)

    return xf.reshape(batch, seq, d)
```

```python
import functools
import math

import jax
import jax.numpy as jnp
import numpy as np
from jax import lax
from jax.experimental import pallas as pl
from jax.experimental.pallas import tpu as pltpu

F32 = jnp.float32
BF16 = jnp.bfloat16

D_MODEL = 1024
D_FF = 2816
D_RNN = 1024
RNN_BLOCKS = 16
RNN_BLOCK_W = D_RNN // RNN_BLOCKS
RNN_CONV_W = 4
RG_LRU_C = 8.0
D_SCONV = 1024
SCONV_W = 3
N_HEADS = 8
N_KV_HEADS = 2
HEAD_DIM = 128
N_IDX_HEADS = 8
IDX_DIM = 64
INDEX_TOPK_MAX = 256
ROPE_THETA = 500000.0
ROPE_FRACTION_DEN = 4
N_BRANCHES = 3
RMS_EPS = 1e-6

LANES = 128
SUBLANES = 8
VMEM_LIMIT = 56 * 1024 * 1024

FFN_TM = 1024
FFN_TF = 256
RNN_TM = 256
SCONV_TM = 512
QKV_TM = 512
MERGE_TM = 512
TAB_TM = 1024
ATT_TQ = 256
ATT_TK = 512
ATT_RB = 128

GATE_GROUP = 256
N_GATE_GROUPS = D_RNN // GATE_GROUP

NEG_BIG = -1e30
LOGIT_SCALE = (HEAD_DIM ** -0.5) * math.log2(math.e)
KEY_NEG_INF = np.int32(np.uint32(0x807FFFFF).view(np.int32))


def _cparams(sem, flags=None):
    return pltpu.CompilerParams(dimension_semantics=sem, vmem_limit_bytes=VMEM_LIMIT, flags=flags)


def _rmsnorm_f32(x, g):
    var = jnp.mean(x * x, axis=-1, keepdims=True)
    return (x * lax.rsqrt(var + RMS_EPS)) * g


def _bdot(a, b):
    return jnp.dot(a, b, preferred_element_type=F32)


def _dot_nt(a, b):
    return lax.dot_general(a, b, (((1,), (1,)), ((), ())), preferred_element_type=F32)


def _ffn_body(x_ref, g_ref, wg_ref, wu_ref, wd_ref, fn_ref, o_ref, xn_ref, acc_ref, *, n_ff, final_norm):
    j = pl.program_id(1)

    @pl.when(j == 0)
    def _():
        xn_ref[...] = _rmsnorm_f32(x_ref[...], g_ref[...]).astype(BF16)
        acc_ref[...] = jnp.zeros_like(acc_ref)

    xn = xn_ref[...]
    g = _bdot(xn, wg_ref[...])
    u = _bdot(xn, wu_ref[...])
    a = (g * jax.nn.sigmoid(g) * u).astype(BF16)
    acc_ref[...] += _bdot(a, wd_ref[...])

    @pl.when(j == n_ff - 1)
    def _():
        y = x_ref[...] + 0.5 * acc_ref[...]
        if final_norm:
            y = _rmsnorm_f32(y, fn_ref[...])
        o_ref[...] = y


def _ffn(x, gain, w_gu, w_d, fnorm, final_norm):
    n = x.shape[0]
    n_ff = D_FF // FFN_TF
    grid = (n // FFN_TM, n_ff)
    return pl.pallas_call(
        functools.partial(_ffn_body, n_ff=n_ff, final_norm=final_norm),
        out_shape=jax.ShapeDtypeStruct((n, D_MODEL), F32),
        grid=grid,
        in_specs=[
            pl.BlockSpec((FFN_TM, D_MODEL), lambda i, j: (i, 0)),
            pl.BlockSpec((1, D_MODEL), lambda i, j: (0, 0)),
            pl.BlockSpec((D_MODEL, FFN_TF), lambda i, j: (0, j)),
            pl.BlockSpec((D_MODEL, FFN_TF), lambda i, j: (0, j + n_ff)),
            pl.BlockSpec((FFN_TF, D_MODEL), lambda i, j: (j, 0)),
            pl.BlockSpec((1, D_MODEL), lambda i, j: (0, 0)),
        ],
        out_specs=pl.BlockSpec((FFN_TM, D_MODEL), lambda i, j: (i, 0)),
        scratch_shapes=[pltpu.VMEM((FFN_TM, D_MODEL), BF16), pltpu.VMEM((FFN_TM, D_MODEL), F32)],
        compiler_params=_cparams(("arbitrary", "arbitrary")),
        name="ffn",
    )(x, gain, w_gu, w_gu, w_d, fnorm)


def _shift_rows(x, d, fill):
    tm = x.shape[0]
    if d % SUBLANES == 0:
        head = jnp.full((d, x.shape[1]), fill, x.dtype)
        return jnp.concatenate([head, x[: tm - d]], axis=0)
    row = lax.broadcasted_iota(jnp.int32, x.shape, 0)
    return jnp.where(row >= d, pltpu.roll(x, d, 0), fill)


def _rnn_body(x_ref, g_ref, w_ref, cw_ref, cb_ref, wbd_ref, ba_ref, bx_ref, lam_ref, wout_ref,
              o_ref, xbuf_ref, h_ref, *, tiles_per_seq):
    tm = x_ref.shape[0]
    i = pl.program_id(0)

    @pl.when(i % tiles_per_seq == 0)
    def _():
        xbuf_ref[0:SUBLANES, :] = jnp.zeros((SUBLANES, D_RNN), F32)
        h_ref[...] = jnp.zeros_like(h_ref)

    u = _rmsnorm_f32(x_ref[...], g_ref[...]).astype(BF16)
    p = _bdot(u, w_ref[...])
    rx = p[:, :D_RNN]
    rg = p[:, D_RNN:]

    xbuf_ref[SUBLANES:, :] = rx
    xa = cb_ref[...] + rx * cw_ref[RNN_CONV_W - 1:RNN_CONV_W, :]
    for k in range(RNN_CONV_W - 1):
        off = SUBLANES - (RNN_CONV_W - 1) + k
        xa = xa + xbuf_ref[off:off + tm, :] * cw_ref[k:k + 1, :]
    xbuf_ref[0:SUBLANES, :] = rx[tm - SUBLANES:, :]

    xab = xa.astype(BF16)
    r_parts, i_parts = [], []
    for jg in range(N_GATE_GROUPS):
        gj = _bdot(xab[:, jg * GATE_GROUP:(jg + 1) * GATE_GROUP], wbd_ref[jg])
        r_parts.append(gj[:, :GATE_GROUP])
        i_parts.append(gj[:, GATE_GROUP:])
    r = jax.nn.sigmoid(jnp.concatenate(r_parts, axis=1) + ba_ref[...])
    ig = jax.nn.sigmoid(jnp.concatenate(i_parts, axis=1) + bx_ref[...])

    nl = -lam_ref[...]
    sp = jnp.maximum(nl, 0.0) + jnp.log1p(jnp.exp(-jnp.abs(nl)))
    log_a = (-RG_LRU_C) * r * sp
    a = jnp.exp(log_a)
    in_scale = jnp.sqrt(-jnp.tanh(log_a) * (a * a + 1.0))
    b = in_scale * (ig * xa)

    d = 1
    while d < tm:
        a_s = _shift_rows(a, d, 1.0)
        b_s = _shift_rows(b, d, 0.0)
        b = a * b_s + b
        a = a * a_s
        d *= 2
    h = b + a * h_ref[...]
    h_ref[...] = h[tm - 1:tm, :]

    gl = 0.5 * rg * (1.0 + jnp.tanh(math.sqrt(2.0 / math.pi) * (rg + 0.044715 * (rg * rg * rg))))
    o_ref[...] = _bdot((h * gl).astype(BF16), wout_ref[...])


def _rnn_branch(x, gain, w_rnn, conv_w, conv_b, wbd, ba, bx, lam, w_out, seq):
    n = x.shape[0]
    tm = RNN_TM
    full = lambda shape: pl.BlockSpec(shape, lambda i: (0,) * len(shape))
    return pl.pallas_call(
        functools.partial(_rnn_body, tiles_per_seq=seq // tm),
        out_shape=jax.ShapeDtypeStruct((n, D_MODEL), F32),
        grid=(n // tm,),
        in_specs=[
            pl.BlockSpec((tm, D_MODEL), lambda i: (i, 0)),
            full((1, D_MODEL)),
            full((D_MODEL, 2 * D_RNN)),
            full((RNN_CONV_W, D_RNN)),
            full((1, D_RNN)),
            full((N_GATE_GROUPS, GATE_GROUP, 2 * GATE_GROUP)),
            full((1, D_RNN)),
            full((1, D_RNN)),
            full((1, D_RNN)),
            full((D_RNN, D_MODEL)),
        ],
        out_specs=pl.BlockSpec((tm, D_MODEL), lambda i: (i, 0)),
        scratch_shapes=[pltpu.VMEM((tm + SUBLANES, D_RNN), F32), pltpu.VMEM((1, D_RNN), F32)],
        compiler_params=_cparams(("arbitrary",)),
        name="rnn_branch",
    )(x, gain, w_rnn, conv_w, conv_b, wbd, ba, bx, lam, w_out)


def _sconv_body(x_ref, g_ref, w_ref, sw_ref, wout_ref, o_ref, zbuf_ref, *, tiles_per_seq):
    tm = x_ref.shape[0]
    i = pl.program_id(0)

    @pl.when(i % tiles_per_seq == 0)
    def _():
        zbuf_ref[0:SUBLANES, :] = jnp.zeros((SUBLANES, D_SCONV), F32)

    u = _rmsnorm_f32(x_ref[...], g_ref[...]).astype(BF16)
    p = _bdot(u, w_ref[...])
    c_b = p[:, :D_SCONV]
    z = p[:, D_SCONV:2 * D_SCONV] * p[:, 2 * D_SCONV:]
    zbuf_ref[SUBLANES:, :] = z
    cv = z * sw_ref[SCONV_W - 1:SCONV_W, :]
    for k in range(SCONV_W - 1):
        off = SUBLANES - (SCONV_W - 1) + k
        cv = cv + zbuf_ref[off:off + tm, :] * sw_ref[k:k + 1, :]
    zbuf_ref[0:SUBLANES, :] = z[tm - SUBLANES:, :]
    o_ref[...] = _bdot((c_b * cv).astype(BF16), wout_ref[...])


def _sconv_branch(x, gain, w_bch, sw, w_out, seq):
    n = x.shape[0]
    tm = SCONV_TM
    full = lambda shape: pl.BlockSpec(shape, lambda i: (0,) * len(shape))
    return pl.pallas_call(
        functools.partial(_sconv_body, tiles_per_seq=seq // tm),
        out_shape=jax.ShapeDtypeStruct((n, D_MODEL), F32),
        grid=(n // tm,),
        in_specs=[
            pl.BlockSpec((tm, D_MODEL), lambda i: (i, 0)),
            full((1, D_MODEL)),
            full((D_MODEL, 3 * D_SCONV)),
            full((SCONV_W, D_SCONV)),
            full((D_SCONV, D_MODEL)),
        ],
        out_specs=pl.BlockSpec((tm, D_MODEL), lambda i: (i, 0)),
        scratch_shapes=[pltpu.VMEM((tm + SUBLANES, D_SCONV), F32)],
        compiler_params=_cparams(("arbitrary",)),
        name="sconv_branch",
    )(x, gain, w_bch, sw, w_out)


def _tab_body(pos_ref, f128_ref, s128_ref, f64_ref, s64_ref, c128_ref, sg128_ref, c64_ref, sg64_ref):
    posf = pos_ref[...].astype(F32)
    for f_ref, s_ref, c_out, s_out in ((f128_ref, s128_ref, c128_ref, sg128_ref),
                                       (f64_ref, s64_ref, c64_ref, sg64_ref)):
        sgn = s_ref[...]
        ang = posf * f_ref[...]
        c_out[...] = jnp.where(sgn != 0.0, jnp.cos(ang), 1.0)
        s_out[...] = jnp.sin(ang) * sgn


def _rope_lane_rows(period, rot):
    half = rot // 2
    inv_freq = ROPE_THETA ** (-jnp.arange(0, rot, 2, dtype=F32) / rot)
    lane = np.arange(LANES) % period
    f_row = jnp.where(lane < rot, inv_freq[lane % half], 0.0).astype(F32)
    s_row = np.where(lane < half, -1.0, np.where(lane < rot, 1.0, 0.0)).astype(np.float32)
    return f_row.reshape(1, LANES), jnp.asarray(s_row).reshape(1, LANES)


def _rope_tables(pos_col):
    n = pos_col.shape[0]
    f128, s128 = _rope_lane_rows(HEAD_DIM, HEAD_DIM // ROPE_FRACTION_DEN)
    f64, s64 = _rope_lane_rows(IDX_DIM, IDX_DIM // ROPE_FRACTION_DEN)
    row = pl.BlockSpec((1, LANES), lambda i: (0, 0))
    tab = pl.BlockSpec((TAB_TM, LANES), lambda i: (i, 0))
    return pl.pallas_call(
        _tab_body,
        out_shape=[jax.ShapeDtypeStruct((n, LANES), F32)] * 4,
        grid=(n // TAB_TM,),
        in_specs=[pl.BlockSpec((TAB_TM, 1), lambda i: (i, 0)), row, row, row, row],
        out_specs=[tab] * 4,
        compiler_params=_cparams(("arbitrary",)),
        name="rope_tables",
    )(pos_col, f128, s128, f64, s64)


def _rope_apply(xh, c, sg, first_half, half):
    partner = jnp.where(first_half, pltpu.roll(xh, LANES - half, 1), pltpu.roll(xh, half, 1))
    return xh * c + partner * sg


def _qkv_body(x_ref, g_ref, w_ref, wt_ref, c128_ref, s128_ref, c64_ref, s64_ref,
              q_ref, k_ref, qi_ref, ki_ref, vt_ref, wit_ref):
    tm = x_ref.shape[0]
    u = _rmsnorm_f32(x_ref[...], g_ref[...]).astype(BF16)
    p = _bdot(u, w_ref[...])
    pt = _dot_nt(wt_ref[...], u)
    kvw = N_KV_HEADS * HEAD_DIM
    vt_ref[...] = pt[:kvw, :].astype(BF16)
    wit_ref[...] = pt[kvw:kvw + N_IDX_HEADS, :]
    lane = lax.broadcasted_iota(jnp.int32, (tm, LANES), 1)
    h128 = HEAD_DIM // ROPE_FRACTION_DEN // 2
    h64 = IDX_DIM // ROPE_FRACTION_DEN // 2
    first128 = lane < h128
    first64 = (lane % IDX_DIM) < h64
    c128, s128 = c128_ref[...], s128_ref[...]
    c64, s64 = c64_ref[...], s64_ref[...]

    off = 0
    for h in range(N_HEADS):
        blk = p[:, off + h * LANES: off + (h + 1) * LANES]
        roped = _rope_apply(blk, c128, s128, first128, h128)
        q_ref[:, h * LANES:(h + 1) * LANES] = (roped * LOGIT_SCALE).astype(BF16)
    off += N_HEADS * HEAD_DIM
    for h in range(N_KV_HEADS):
        blk = p[:, off + h * LANES: off + (h + 1) * LANES]
        k_ref[:, h * LANES:(h + 1) * LANES] = _rope_apply(blk, c128, s128, first128, h128).astype(BF16)
    off += N_KV_HEADS * HEAD_DIM
    for h in range(N_IDX_HEADS * IDX_DIM // LANES):
        blk = p[:, off + h * LANES: off + (h + 1) * LANES]
        qi_ref[:, h * LANES:(h + 1) * LANES] = _rope_apply(blk, c64, s64, first64, h64).astype(BF16)
    off += N_IDX_HEADS * IDX_DIM
    blk = p[:, off: off + LANES]
    ki_ref[...] = _rope_apply(blk, c64, s64, first64, h64)[:, :IDX_DIM].astype(BF16)


def _qkv_proj(x, gain, w_qkv, w_t, tabs):
    n = x.shape[0]
    tm = QKV_TM
    full = lambda shape: pl.BlockSpec(shape, lambda i: (0,) * len(shape))
    tile = lambda w: pl.BlockSpec((tm, w), lambda i: (i, 0))
    ttile = lambda r: pl.BlockSpec((r, tm), lambda i: (0, i))
    kvw = N_KV_HEADS * HEAD_DIM
    return pl.pallas_call(
        _qkv_body,
        out_shape=[
            jax.ShapeDtypeStruct((n, N_HEADS * HEAD_DIM), BF16),
            jax.ShapeDtypeStruct((n, kvw), BF16),
            jax.ShapeDtypeStruct((n, N_IDX_HEADS * IDX_DIM), BF16),
            jax.ShapeDtypeStruct((n, IDX_DIM), BF16),
            jax.ShapeDtypeStruct((kvw, n), BF16),
            jax.ShapeDtypeStruct((N_IDX_HEADS, n), F32),
        ],
        grid=(n // tm,),
        in_specs=[tile(D_MODEL), full((1, D_MODEL)), full(w_qkv.shape), full(w_t.shape),
                  tile(LANES), tile(LANES), tile(LANES), tile(LANES)],
        out_specs=[tile(N_HEADS * HEAD_DIM), tile(kvw), tile(N_IDX_HEADS * IDX_DIM), tile(IDX_DIM),
                   ttile(kvw), ttile(N_IDX_HEADS)],
        compiler_params=_cparams(("arbitrary",)),
        name="qkv_proj",
    )(x, gain, w_qkv, w_t, *tabs)


def _key_to_float(u):
    key = u + KEY_NEG_INF
    bits = jnp.where(key < 0, key ^ jnp.int32(0x7FFFFFFF), key)
    return lax.bitcast_convert_type(bits, F32)


def _attn_body(q_ref, qi_ref, wit_ref, ki_ref, k_ref, vt_ref, o_ref,
               sc_ref, m_ref, l_ref, acc_ref, cut_ref, bias_ref, lg_ref, p_ref, *, seq, top_k):
    tq, tk = ATT_TQ, ATT_TK
    j = pl.program_id(1)
    q0 = j * tq
    nch = (q0 + tq - 1) // tk + 1
    tpos = q0 + lax.broadcasted_iota(jnp.int32, (1, tq), 1)
    w_scale = (N_IDX_HEADS ** -0.5) * (IDX_DIM ** -0.5)

    def chunk_rows(c):
        return pl.ds(pl.multiple_of(c * tk, tk), tk)

    def key_pos(c):
        return c * tk + lax.broadcasted_iota(jnp.int32, (tk, tq), 0)

    def fold_rows(x):
        groups = tk // SUBLANES
        side = int(math.isqrt(groups))
        x4 = x.reshape(groups // side, side, SUBLANES, tq)
        return jnp.sum(jnp.sum(x4, axis=1), axis=0)

    wsc = wit_ref[...] * w_scale

    def score_chunk(c, carry):
        kic = ki_ref[chunk_rows(c), :]
        acc = jnp.zeros((tk, tq), F32)
        for h in range(N_IDX_HEADS):
            rel = _dot_nt(kic, qi_ref[:, h * IDX_DIM:(h + 1) * IDX_DIM])
            acc = acc + wsc[h:h + 1, :] * jnp.maximum(rel, 0.0)
        sc_ref[chunk_rows(c), :] = jnp.where(key_pos(c) <= tpos, acc, -jnp.inf)
        return carry

    lax.fori_loop(0, nch, score_chunk, 0)

    def count(pred):
        def body(c, acc):
            return acc + fold_rows(jnp.where(pred(sc_ref[chunk_rows(c), :], c), 1.0, 0.0))
        acc = lax.fori_loop(0, nch, body, jnp.zeros((SUBLANES, tq), F32))
        return jnp.sum(acc, axis=0, keepdims=True)

    kf = float(top_k)

    def radix_step(it, u):
        cand = u | lax.shift_left(jnp.int32(1), 31 - it)
        cf = _key_to_float(cand)
        cnt = count(lambda s, c: s >= cf)
        return jnp.where(cnt >= kf, cand, u)

    u_thr = lax.fori_loop(0, 32, radix_step, jnp.zeros((1, tq), jnp.int32))
    thr = _key_to_float(u_thr)

    cnt_gt = count(lambda s, c: s > thr)
    cnt_ge = count(lambda s, c: s >= thr)
    need = kf - cnt_gt
    tie = (cnt_ge > kf) & (thr > -jnp.inf)
    cut_ref[...] = jnp.full(cut_ref.shape, seq, jnp.int32)

    @pl.when(jnp.sum(jnp.where(tie, 1.0, 0.0)) > 0.0)
    def _():
        def idx_step(it, v):
            cand = v | lax.shift_left(jnp.int32(1), (seq.bit_length() - 2) - it)
            cnt = count(lambda s, c: (s == thr) & (key_pos(c) < cand))
            return jnp.where(cnt < need, cand, v)
        v = lax.fori_loop(0, seq.bit_length() - 1, idx_step, jnp.zeros((1, tq), jnp.int32))
        cut_ref[...] = jnp.broadcast_to(jnp.where(tie, v, seq), cut_ref.shape)

    cut = cut_ref[0:1, :]

    m_ref[...] = jnp.full(m_ref.shape, NEG_BIG, F32)
    l_ref[...] = jnp.zeros_like(l_ref)
    acc_ref[...] = jnp.zeros_like(acc_ref)
    hpg = N_HEADS // N_KV_HEADS

    rb_rows = ATT_RB
    nrb = tk // rb_rows
    pv_rows = 2 * rb_rows

    def attn_chunk(c, carry):
        def rows(rb, n=rb_rows):
            return pl.ds(pl.multiple_of(c * tk + rb * rb_rows, rb_rows), n)

        def blk(rb, n=rb_rows):
            return slice(rb * rb_rows, rb * rb_rows + n)

        def fold(x, op):
            return op(x.reshape(rb_rows // SUBLANES, SUBLANES, tq), axis=0)

        def qk_block(h, rb, bias):
            g = h // hpg
            kc = k_ref[rows(rb), g * HEAD_DIM:(g + 1) * HEAD_DIM]
            return _dot_nt(kc, q_ref[:, h * HEAD_DIM:(h + 1) * HEAD_DIM]) + bias

        def pv_half(h, slot, half, alpha):
            g = h // hpg
            cols = pl.ds(pl.multiple_of(c * tk + half * pv_rows, pv_rows), pv_rows)
            vtc = vt_ref[g * HEAD_DIM:(g + 1) * HEAD_DIM, cols]
            t = _bdot(vtc, p_ref[slot, blk(half * 2, pv_rows), :])
            if half == 0:
                acc_ref[h] = alpha * acc_ref[h] + t
            else:
                acc_ref[h] = acc_ref[h] + t

        cm = None
        for rb in range(nrb):
            s = sc_ref[rows(rb), :]
            kp = c * tk + rb * rb_rows + lax.broadcasted_iota(jnp.int32, (rb_rows, tq), 0)
            sel = (kp <= tpos) & ((s > thr) | ((s == thr) & (kp <= cut)))
            bias = jnp.where(sel, 0.0, NEG_BIG)
            bias_ref[blk(rb), :] = bias
            lg = qk_block(0, rb, bias)
            lg_ref[0, blk(rb), :] = lg
            part = fold(lg, jnp.max)
            cm = part if cm is None else jnp.maximum(cm, part)

        alpha_prev = None
        for h in range(N_HEADS):
            cur = h % 2
            m_prev = m_ref[h]
            m_new = jnp.maximum(m_prev, jnp.max(cm, axis=0, keepdims=True))
            alpha = jnp.exp2(m_prev - m_new)
            m_ref[h] = m_new
            cm = None
            lsum = None
            for rb in range(nrb):
                if h + 1 < N_HEADS:
                    lg = qk_block(h + 1, rb, bias_ref[blk(rb), :])
                    lg_ref[1 - cur, blk(rb), :] = lg
                    part = fold(lg, jnp.max)
                    cm = part if cm is None else jnp.maximum(cm, part)
                pr = jnp.exp2(lg_ref[cur, blk(rb), :] - m_new)
                part = fold(pr, jnp.sum)
                lsum = part if lsum is None else lsum + part
                p_ref[cur, blk(rb), :] = pr.astype(BF16)
                if h > 0 and rb % 2 == 0 and rb // 2 < tk // pv_rows:
                    pv_half(h - 1, 1 - cur, rb // 2, alpha_prev)
            l_ref[h] = alpha * l_ref[h] + lsum
            alpha_prev = alpha
        last = N_HEADS - 1
        for half in range(tk // pv_rows):
            pv_half(last, last % 2, half, alpha_prev)
        return carry

    lax.fori_loop(0, nch, attn_chunk, 0)

    for h in range(N_HEADS):
        out_t = acc_ref[h] / jnp.sum(l_ref[h], axis=0, keepdims=True)
        o_ref[:, h * HEAD_DIM:(h + 1) * HEAD_DIM] = out_t.T.astype(BF16)


def _attention(q, qi, wit, ki, k, vt, batch, seq):
    n = q.shape[0]
    tq, tk = ATT_TQ, ATT_TK
    nq = seq // tq
    top_k = min(INDEX_TOPK_MAX, seq // 4)
    kvw = N_KV_HEADS * HEAD_DIM
    qtile = lambda w: pl.BlockSpec((tq, w), lambda b, j: (b * nq + j, 0))
    whole = lambda w: pl.BlockSpec((seq, w), lambda b, j: (b, 0))
    return pl.pallas_call(
        functools.partial(_attn_body, seq=seq, top_k=top_k),
        out_shape=jax.ShapeDtypeStruct((n, N_HEADS * HEAD_DIM), BF16),
        grid=(batch, nq),
        in_specs=[qtile(N_HEADS * HEAD_DIM), qtile(N_IDX_HEADS * IDX_DIM),
                  pl.BlockSpec((N_IDX_HEADS, tq), lambda b, j: (0, b * nq + j)),
                  whole(IDX_DIM), whole(kvw),
                  pl.BlockSpec((kvw, seq), lambda b, j: (0, b))],
        out_specs=qtile(N_HEADS * HEAD_DIM),
        scratch_shapes=[
            pltpu.VMEM((seq, tq), F32),
            pltpu.VMEM((N_HEADS, 1, tq), F32),
            pltpu.VMEM((N_HEADS, SUBLANES, tq), F32),
            pltpu.VMEM((N_HEADS, HEAD_DIM, tq), F32),
            pltpu.VMEM((SUBLANES, tq), jnp.int32),
            pltpu.VMEM((tk, tq), F32),
            pltpu.VMEM((2, tk, tq), F32),
            pltpu.VMEM((2, tk, tq), BF16),
        ],
        compiler_params=_cparams(("arbitrary", "arbitrary")),
        name="dsa_attention",
    )(q, qi, wit, ki, k, vt)


def _merge_body(x_ref, g_ref, wg_ref, ya_ref, yb_ref, oc_ref, wao_ref, wo_ref, o_ref):
    x = x_ref[...]
    u = _rmsnorm_f32(x, g_ref[...]).astype(BF16)
    gt = jax.nn.sigmoid(_bdot(u, wg_ref[...]))
    yc = _bdot(oc_ref[...], wao_ref[...])
    merged = (gt[:, :D_MODEL] * ya_ref[...] + gt[:, D_MODEL:2 * D_MODEL] * yb_ref[...]
              + gt[:, 2 * D_MODEL:] * yc)
    o_ref[...] = x + _bdot(merged.astype(BF16), wo_ref[...])


def _merge(x, gain, w_gates, ya, yb, oc, w_attn_out, w_o):
    n = x.shape[0]
    tm = MERGE_TM
    full = lambda shape: pl.BlockSpec(shape, lambda i: (0,) * len(shape))
    tile = pl.BlockSpec((tm, D_MODEL), lambda i: (i, 0))
    return pl.pallas_call(
        _merge_body,
        out_shape=jax.ShapeDtypeStruct((n, D_MODEL), F32),
        grid=(n // tm,),
        in_specs=[tile, full((1, D_MODEL)), full((D_MODEL, N_BRANCHES * D_MODEL)), tile, tile, tile,
                  full((N_HEADS * HEAD_DIM, D_MODEL)), full((D_MODEL, D_MODEL))],
        out_specs=tile,
        compiler_params=_cparams(("arbitrary",)),
        name="merge",
    )(x, gain, w_gates, ya, yb, oc, w_attn_out, w_o)


def _block_diag_groups(w):
    per = GATE_GROUP // RNN_BLOCK_W
    w4 = w.reshape(N_GATE_GROUPS, per, RNN_BLOCK_W, RNN_BLOCK_W)
    eye = jnp.eye(per, dtype=w.dtype)
    return jnp.einsum("gnij,nm->gnimj", w4, eye).reshape(N_GATE_GROUPS, GATE_GROUP, GATE_GROUP)


def kernel(x, positions, ffn1_norm, ffn1_w_gate_up, ffn1_w_down, mix_norm, w_in, rnn_conv_w, rnn_conv_b,
           rnn_gate_a_w, rnn_gate_a_b, rnn_gate_x_w, rnn_gate_x_b, rnn_lambda, rnn_w_out, sconv_w,
           sconv_w_out, attn_w_out, w_o, ffn2_norm, ffn2_w_gate_up, ffn2_w_down, final_norm):
    batch, seq, d = x.shape
    depth = w_in.shape[0]
    n = batch * seq
    xf = x.reshape(n, d)
    row = lambda v: v.reshape(1, -1)

    tabs = _rope_tables(positions.reshape(n, 1))

    o_rnn = 0
    o_bch = o_rnn + 2 * D_RNN
    o_q = o_bch + 3 * D_SCONV
    o_k = o_q + N_HEADS * HEAD_DIM
    o_v = o_k + N_KV_HEADS * HEAD_DIM
    o_qi = o_v + N_KV_HEADS * HEAD_DIM
    o_ki = o_qi + N_IDX_HEADS * IDX_DIM
    o_wi = o_ki + IDX_DIM
    o_gates = o_wi + N_IDX_HEADS

    for l in range(depth):
        win = w_in[l]
        w_rnn = win[:, o_rnn:o_bch].astype(BF16)
        w_bch = win[:, o_bch:o_q].astype(BF16)
        w_qkv = jnp.concatenate([win[:, o_q:o_v], win[:, o_qi:o_wi],
                                 jnp.zeros((d, LANES - IDX_DIM), win.dtype)], axis=1).astype(BF16)
        w_t = jnp.concatenate([win[:, o_v:o_qi], win[:, o_wi:o_gates]], axis=1).T.astype(BF16)
        w_gates = win[:, o_gates:].astype(BF16)
        wbd = jnp.concatenate([_block_diag_groups(rnn_gate_a_w[l]), _block_diag_groups(rnn_gate_x_w[l])],
                              axis=-1).astype(BF16)

        xf = _ffn(xf, row(ffn1_norm[l]), ffn1_w_gate_up[l].astype(BF16), ffn1_w_down[l].astype(BF16),
                  row(final_norm), False)

        gain = row(mix_norm[l])
        ya = _rnn_branch(xf, gain, w_rnn, rnn_conv_w[l], row(rnn_conv_b[l]), wbd, row(rnn_gate_a_b[l]),
                         row(rnn_gate_x_b[l]), row(rnn_lambda[l]), rnn_w_out[l].astype(BF16), seq)
        yb = _sconv_branch(xf, gain, w_bch, sconv_w[l], sconv_w_out[l].astype(BF16), seq)
        q, k, qi, ki, vt, wit = _qkv_proj(xf, gain, w_qkv, w_t, tabs)
        oc = _attention(q, qi, wit, ki, k, vt, batch, seq)
        xf = _merge(xf, gain, w_gates, ya, yb, oc, attn_w_out[l].astype(BF16), w_o[l].astype(BF16))

        xf = _ffn(xf, row(ffn2_norm[l]), ffn2_w_gate_up[l].astype(BF16), ffn2_w_down[l].astype(BF16),
                  row(final_norm), l == depth - 1)

    return xf.reshape(batch, seq, d)
```

```python
import functools
import math

import jax
import jax.numpy as jnp
import numpy as np
from jax import lax
from jax.experimental import pallas as pl
from jax.experimental.pallas import tpu as pltpu

F32 = jnp.float32
BF16 = jnp.bfloat16

D_MODEL = 1024
D_FF = 2816
D_RNN = 1024
RNN_BLOCKS = 16
RNN_BLOCK_W = D_RNN // RNN_BLOCKS
RNN_CONV_W = 4
RG_LRU_C = 8.0
D_SCONV = 1024
SCONV_W = 3
N_HEADS = 8
N_KV_HEADS = 2
HEAD_DIM = 128
N_IDX_HEADS = 8
IDX_DIM = 64
INDEX_TOPK_MAX = 256
ROPE_THETA = 500000.0
ROPE_FRACTION_DEN = 4
N_BRANCHES = 3
RMS_EPS = 1e-6

LANES = 128
SUBLANES = 8
VMEM_LIMIT = 56 * 1024 * 1024

FFN_TM = 1024
FFN_TF = 256
RNN_TM = 256
SCONV_TM = 512
QKV_TM = 512
MERGE_TM = 512
TAB_TM = 1024
ATT_TQ = 256
ATT_TK = 512
ATT_RB = 128

GATE_GROUP = 256
N_GATE_GROUPS = D_RNN // GATE_GROUP

NEG_BIG = -1e30
LOGIT_SCALE = (HEAD_DIM ** -0.5) * math.log2(math.e)
KEY_NEG_INF = np.int32(np.uint32(0x807FFFFF).view(np.int32))


def _cparams(sem, flags=None):
    return pltpu.CompilerParams(dimension_semantics=sem, vmem_limit_bytes=VMEM_LIMIT, flags=flags)


def _rmsnorm_f32(x, g):
    var = jnp.mean(x * x, axis=-1, keepdims=True)
    return (x * lax.rsqrt(var + RMS_EPS)) * g


def _bdot(a, b):
    return jnp.dot(a, b, preferred_element_type=F32)


def _dot_nt(a, b):
    return lax.dot_general(a, b, (((1,), (1,)), ((), ())), preferred_element_type=F32)


def _ffn_body(x_ref, g_ref, wg_ref, wu_ref, wd_ref, fn_ref, o_ref, xn_ref, acc_ref, *, n_ff, final_norm):
    j = pl.program_id(1)

    @pl.when(j == 0)
    def _():
        xn_ref[...] = _rmsnorm_f32(x_ref[...], g_ref[...]).astype(BF16)
        acc_ref[...] = jnp.zeros_like(acc_ref)

    xn = xn_ref[...]
    g = _bdot(xn, wg_ref[...])
    u = _bdot(xn, wu_ref[...])
    a = (g * jax.nn.sigmoid(g) * u).astype(BF16)
    acc_ref[...] += _bdot(a, wd_ref[...])

    @pl.when(j == n_ff - 1)
    def _():
        y = x_ref[...] + 0.5 * acc_ref[...]
        if final_norm:
            y = _rmsnorm_f32(y, fn_ref[...])
        o_ref[...] = y


def _ffn(x, gain, w_gu, w_d, fnorm, final_norm):
    n = x.shape[0]
    n_ff = D_FF // FFN_TF
    grid = (n // FFN_TM, n_ff)
    return pl.pallas_call(
        functools.partial(_ffn_body, n_ff=n_ff, final_norm=final_norm),
        out_shape=jax.ShapeDtypeStruct((n, D_MODEL), F32),
        grid=grid,
        in_specs=[
            pl.BlockSpec((FFN_TM, D_MODEL), lambda i, j: (i, 0)),
            pl.BlockSpec((1, D_MODEL), lambda i, j: (0, 0)),
            pl.BlockSpec((D_MODEL, FFN_TF), lambda i, j: (0, j)),
            pl.BlockSpec((D_MODEL, FFN_TF), lambda i, j: (0, j + n_ff)),
            pl.BlockSpec((FFN_TF, D_MODEL), lambda i, j: (j, 0)),
            pl.BlockSpec((1, D_MODEL), lambda i, j: (0, 0)),
        ],
        out_specs=pl.BlockSpec((FFN_TM, D_MODEL), lambda i, j: (i, 0)),
        scratch_shapes=[pltpu.VMEM((FFN_TM, D_MODEL), BF16), pltpu.VMEM((FFN_TM, D_MODEL), F32)],
        compiler_params=_cparams(("arbitrary", "arbitrary")),
        name="ffn",
    )(x, gain, w_gu, w_gu, w_d, fnorm)


def _causal_dwconv_tile(x, prev_rows, w_ref, width):
    tm, ch = x.shape
    groups = tm // SUBLANES
    x3 = x.reshape(groups, SUBLANES, ch)
    sub = lax.broadcasted_iota(jnp.int32, x3.shape, 1)
    out = x3 * w_ref[width - 1:width, :]
    for lag in range(1, width):
        cur = pltpu.roll(x3, lag, 1)
        head = pltpu.roll(prev_rows.reshape(1, SUBLANES, ch), lag, 1)
        prev = jnp.concatenate([head, cur[:groups - 1]], axis=0)
        out = out + jnp.where(sub >= lag, cur, prev) * w_ref[width - 1 - lag:width - lag, :]
    return out.reshape(tm, ch)


def _rnn_body(x_ref, g_ref, w_ref, cw_ref, cb_ref, wbd_ref, ba_ref, bx_ref, lam_ref, wout_ref,
              o_ref, xbuf_ref, h_ref, *, tiles_per_seq):
    tm = x_ref.shape[0]
    i = pl.program_id(0)

    @pl.when(i % tiles_per_seq == 0)
    def _():
        xbuf_ref[...] = jnp.zeros_like(xbuf_ref)
        h_ref[...] = jnp.zeros_like(h_ref)

    u = _rmsnorm_f32(x_ref[...], g_ref[...]).astype(BF16)
    p = _bdot(u, w_ref[...])
    rx = p[:, :D_RNN]
    rg = p[:, D_RNN:]

    xa = cb_ref[...] + _causal_dwconv_tile(rx, xbuf_ref[...], cw_ref, RNN_CONV_W)
    xbuf_ref[...] = rx[tm - SUBLANES:, :]

    xab = xa.astype(BF16)
    r_parts, i_parts = [], []
    for jg in range(N_GATE_GROUPS):
        gj = _bdot(xab[:, jg * GATE_GROUP:(jg + 1) * GATE_GROUP], wbd_ref[jg])
        r_parts.append(gj[:, :GATE_GROUP])
        i_parts.append(gj[:, GATE_GROUP:])
    r = jax.nn.sigmoid(jnp.concatenate(r_parts, axis=1) + ba_ref[...])
    ig = jax.nn.sigmoid(jnp.concatenate(i_parts, axis=1) + bx_ref[...])

    nl = -lam_ref[...]
    sp = jnp.maximum(nl, 0.0) + jnp.log1p(jnp.exp(-jnp.abs(nl)))
    log_a = (-RG_LRU_C) * r * sp
    a = jnp.exp(log_a)
    one_m_a2 = -jnp.tanh(log_a) * (a * a + 1.0)
    in_scale = jnp.where(one_m_a2 > 0.0, one_m_a2 * lax.rsqrt(one_m_a2), 0.0)
    b = in_scale * (ig * xa)

    groups = tm // SUBLANES
    a3 = a.reshape(groups, SUBLANES, D_RNN)
    b3 = b.reshape(groups, SUBLANES, D_RNN)
    sub = lax.broadcasted_iota(jnp.int32, a3.shape, 1)
    d = 1
    while d < SUBLANES:
        keep = sub >= d
        a_s = jnp.where(keep, pltpu.roll(a3, d, 1), 1.0)
        b_s = jnp.where(keep, pltpu.roll(b3, d, 1), 0.0)
        b3 = a3 * b_s + b3
        a3 = a3 * a_s
        d *= 2
    carry = h_ref[...]
    h_groups = []
    for gi in range(groups):
        hg = b3[gi] + a3[gi] * carry
        h_groups.append(hg)
        carry = hg[SUBLANES - 1:SUBLANES, :]
    h_ref[...] = carry
    h = jnp.concatenate(h_groups, axis=0)

    gl = 0.5 * rg * (1.0 + jnp.tanh(math.sqrt(2.0 / math.pi) * (rg + 0.044715 * (rg * rg * rg))))
    o_ref[...] = _bdot((h * gl).astype(BF16), wout_ref[...])


def _rnn_branch(x, gain, w_rnn, conv_w, conv_b, wbd, ba, bx, lam, w_out, seq):
    n = x.shape[0]
    tm = RNN_TM
    full = lambda shape: pl.BlockSpec(shape, lambda i: (0,) * len(shape))
    return pl.pallas_call(
        functools.partial(_rnn_body, tiles_per_seq=seq // tm),
        out_shape=jax.ShapeDtypeStruct((n, D_MODEL), F32),
        grid=(n // tm,),
        in_specs=[
            pl.BlockSpec((tm, D_MODEL), lambda i: (i, 0)),
            full((1, D_MODEL)),
            full((D_MODEL, 2 * D_RNN)),
            full((RNN_CONV_W, D_RNN)),
            full((1, D_RNN)),
            full((N_GATE_GROUPS, GATE_GROUP, 2 * GATE_GROUP)),
            full((1, D_RNN)),
            full((1, D_RNN)),
            full((1, D_RNN)),
            full((D_RNN, D_MODEL)),
        ],
        out_specs=pl.BlockSpec((tm, D_MODEL), lambda i: (i, 0)),
        scratch_shapes=[pltpu.VMEM((SUBLANES, D_RNN), F32), pltpu.VMEM((1, D_RNN), F32)],
        compiler_params=_cparams(("arbitrary",)),
        name="rnn_branch",
    )(x, gain, w_rnn, conv_w, conv_b, wbd, ba, bx, lam, w_out)


def _sconv_body(x_ref, g_ref, w_ref, sw_ref, wout_ref, o_ref, zbuf_ref, *, tiles_per_seq):
    tm = x_ref.shape[0]
    i = pl.program_id(0)

    @pl.when(i % tiles_per_seq == 0)
    def _():
        zbuf_ref[...] = jnp.zeros_like(zbuf_ref)

    u = _rmsnorm_f32(x_ref[...], g_ref[...]).astype(BF16)
    p = _bdot(u, w_ref[...])
    c_b = p[:, :D_SCONV]
    z = p[:, D_SCONV:2 * D_SCONV] * p[:, 2 * D_SCONV:]
    cv = _causal_dwconv_tile(z, zbuf_ref[...], sw_ref, SCONV_W)
    zbuf_ref[...] = z[tm - SUBLANES:, :]
    o_ref[...] = _bdot((c_b * cv).astype(BF16), wout_ref[...])


def _sconv_branch(x, gain, w_bch, sw, w_out, seq):
    n = x.shape[0]
    tm = SCONV_TM
    full = lambda shape: pl.BlockSpec(shape, lambda i: (0,) * len(shape))
    return pl.pallas_call(
        functools.partial(_sconv_body, tiles_per_seq=seq // tm),
        out_shape=jax.ShapeDtypeStruct((n, D_MODEL), F32),
        grid=(n // tm,),
        in_specs=[
            pl.BlockSpec((tm, D_MODEL), lambda i: (i, 0)),
            full((1, D_MODEL)),
            full((D_MODEL, 3 * D_SCONV)),
            full((SCONV_W, D_SCONV)),
            full((D_SCONV, D_MODEL)),
        ],
        out_specs=pl.BlockSpec((tm, D_MODEL), lambda i: (i, 0)),
        scratch_shapes=[pltpu.VMEM((SUBLANES, D_SCONV), F32)],
        compiler_params=_cparams(("arbitrary",)),
        name="sconv_branch",
    )(x, gain, w_bch, sw, w_out)


def _tab_body(pos_ref, f128_ref, s128_ref, f64_ref, s64_ref, c128_ref, sg128_ref, c64_ref, sg64_ref):
    posf = pos_ref[...].astype(F32)
    for f_ref, s_ref, c_out, s_out in ((f128_ref, s128_ref, c128_ref, sg128_ref),
                                       (f64_ref, s64_ref, c64_ref, sg64_ref)):
        sgn = s_ref[...]
        ang = posf * f_ref[...]
        c_out[...] = jnp.where(sgn != 0.0, jnp.cos(ang), 1.0)
        s_out[...] = jnp.sin(ang) * sgn


def _rope_lane_rows(period, rot):
    half = rot // 2
    inv_freq = ROPE_THETA ** (-jnp.arange(0, rot, 2, dtype=F32) / rot)
    lane = np.arange(LANES) % period
    f_row = jnp.where(lane < rot, inv_freq[lane % half], 0.0).astype(F32)
    s_row = np.where(lane < half, -1.0, np.where(lane < rot, 1.0, 0.0)).astype(np.float32)
    return f_row.reshape(1, LANES), jnp.asarray(s_row).reshape(1, LANES)


def _rope_tables(pos_col):
    n = pos_col.shape[0]
    f128, s128 = _rope_lane_rows(HEAD_DIM, HEAD_DIM // ROPE_FRACTION_DEN)
    f64, s64 = _rope_lane_rows(IDX_DIM, IDX_DIM // ROPE_FRACTION_DEN)
    row = pl.BlockSpec((1, LANES), lambda i: (0, 0))
    tab = pl.BlockSpec((TAB_TM, LANES), lambda i: (i, 0))
    return pl.pallas_call(
        _tab_body,
        out_shape=[jax.ShapeDtypeStruct((n, LANES), F32)] * 4,
        grid=(n // TAB_TM,),
        in_specs=[pl.BlockSpec((TAB_TM, 1), lambda i: (i, 0)), row, row, row, row],
        out_specs=[tab] * 4,
        compiler_params=_cparams(("arbitrary",)),
        name="rope_tables",
    )(pos_col, f128, s128, f64, s64)


def _rope_apply(xh, c, sg, first_half, half):
    partner = jnp.where(first_half, pltpu.roll(xh, LANES - half, 1), pltpu.roll(xh, half, 1))
    return xh * c + partner * sg


def _qkv_body(x_ref, g_ref, w_ref, wt_ref, c128_ref, s128_ref, c64_ref, s64_ref,
              q_ref, k_ref, qi_ref, ki_ref, vt_ref, wit_ref):
    tm = x_ref.shape[0]
    u = _rmsnorm_f32(x_ref[...], g_ref[...]).astype(BF16)
    p = _bdot(u, w_ref[...])
    pt = _dot_nt(wt_ref[...], u)
    kvw = N_KV_HEADS * HEAD_DIM
    vt_ref[...] = pt[:kvw, :].astype(BF16)
    wit_ref[...] = pt[kvw:kvw + N_IDX_HEADS, :]
    lane = lax.broadcasted_iota(jnp.int32, (tm, LANES), 1)
    h128 = HEAD_DIM // ROPE_FRACTION_DEN // 2
    h64 = IDX_DIM // ROPE_FRACTION_DEN // 2
    first128 = lane < h128
    first64 = (lane % IDX_DIM) < h64
    c128, s128 = c128_ref[...], s128_ref[...]
    c64, s64 = c64_ref[...], s64_ref[...]

    off = 0
    for h in range(N_HEADS):
        blk = p[:, off + h * LANES: off + (h + 1) * LANES]
        roped = _rope_apply(blk, c128, s128, first128, h128)
        q_ref[:, h * LANES:(h + 1) * LANES] = (roped * LOGIT_SCALE).astype(BF16)
    off += N_HEADS * HEAD_DIM
    for h in range(N_KV_HEADS):
        blk = p[:, off + h * LANES: off + (h + 1) * LANES]
        k_ref[:, h * LANES:(h + 1) * LANES] = _rope_apply(blk, c128, s128, first128, h128).astype(BF16)
    off += N_KV_HEADS * HEAD_DIM
    for h in range(N_IDX_HEADS * IDX_DIM // LANES):
        blk = p[:, off + h * LANES: off + (h + 1) * LANES]
        qi_ref[:, h * LANES:(h + 1) * LANES] = _rope_apply(blk, c64, s64, first64, h64).astype(BF16)
    off += N_IDX_HEADS * IDX_DIM
    blk = p[:, off: off + LANES]
    ki_ref[...] = _rope_apply(blk, c64, s64, first64, h64)[:, :IDX_DIM].astype(BF16)


def _qkv_proj(x, gain, w_qkv, w_t, tabs):
    n = x.shape[0]
    tm = QKV_TM
    full = lambda shape: pl.BlockSpec(shape, lambda i: (0,) * len(shape))
    tile = lambda w: pl.BlockSpec((tm, w), lambda i: (i, 0))
    ttile = lambda r: pl.BlockSpec((r, tm), lambda i: (0, i))
    kvw = N_KV_HEADS * HEAD_DIM
    return pl.pallas_call(
        _qkv_body,
        out_shape=[
            jax.ShapeDtypeStruct((n, N_HEADS * HEAD_DIM), BF16),
            jax.ShapeDtypeStruct((n, kvw), BF16),
            jax.ShapeDtypeStruct((n, N_IDX_HEADS * IDX_DIM), BF16),
            jax.ShapeDtypeStruct((n, IDX_DIM), BF16),
            jax.ShapeDtypeStruct((kvw, n), BF16),
            jax.ShapeDtypeStruct((N_IDX_HEADS, n), F32),
        ],
        grid=(n // tm,),
        in_specs=[tile(D_MODEL), full((1, D_MODEL)), full(w_qkv.shape), full(w_t.shape),
                  tile(LANES), tile(LANES), tile(LANES), tile(LANES)],
        out_specs=[tile(N_HEADS * HEAD_DIM), tile(kvw), tile(N_IDX_HEADS * IDX_DIM), tile(IDX_DIM),
                   ttile(kvw), ttile(N_IDX_HEADS)],
        compiler_params=_cparams(("arbitrary",)),
        name="qkv_proj",
    )(x, gain, w_qkv, w_t, *tabs)


def _key_to_float(u):
    key = u + KEY_NEG_INF
    bits = jnp.where(key < 0, key ^ jnp.int32(0x7FFFFFFF), key)
    return lax.bitcast_convert_type(bits, F32)


def _attn_body(q_ref, qi_ref, wit_ref, ki_ref, k_ref, vt_ref, tri_ref, o_ref,
               sc_ref, m_ref, l_ref, acc_ref, need_ref, run_ref, bias_ref, lg_ref, p_ref, *, seq, top_k):
    tq, tk = ATT_TQ, ATT_TK
    j = pl.program_id(1)
    q0 = j * tq
    nch = (q0 + tq - 1) // tk + 1
    tpos = q0 + lax.broadcasted_iota(jnp.int32, (1, tq), 1)
    w_scale = (N_IDX_HEADS ** -0.5) * (IDX_DIM ** -0.5)

    def chunk_rows(c):
        return pl.ds(pl.multiple_of(c * tk, tk), tk)

    def key_pos(c):
        return c * tk + lax.broadcasted_iota(jnp.int32, (tk, tq), 0)

    def fold_rows(x):
        groups = tk // SUBLANES
        side = int(math.isqrt(groups))
        x4 = x.reshape(groups // side, side, SUBLANES, tq)
        return jnp.sum(jnp.sum(x4, axis=1), axis=0)

    wsc = wit_ref[...] * w_scale

    def score_chunk(c, carry):
        kic = ki_ref[chunk_rows(c), :]
        acc = jnp.zeros((tk, tq), F32)
        for h in range(N_IDX_HEADS):
            rel = _dot_nt(kic, qi_ref[:, h * IDX_DIM:(h + 1) * IDX_DIM])
            acc = acc + wsc[h:h + 1, :] * jnp.maximum(rel, 0.0)
        sc_ref[chunk_rows(c), :] = jnp.where(key_pos(c) <= tpos, acc, -jnp.inf)
        return carry

    lax.fori_loop(0, nch, score_chunk, 0)

    def count(pred):
        def body(c, acc):
            return acc + fold_rows(jnp.where(pred(sc_ref[chunk_rows(c), :], c), 1.0, 0.0))
        acc = lax.fori_loop(0, nch, body, jnp.zeros((SUBLANES, tq), F32))
        return jnp.sum(acc, axis=0, keepdims=True)

    kf = float(top_k)

    def radix_step(it, carry):
        u, cnt_u = carry
        cand = u | lax.shift_left(jnp.int32(1), 31 - it)
        cf = _key_to_float(cand)
        cnt = count(lambda s, c: s >= cf)
        ok = cnt >= kf
        return jnp.where(ok, cand, u), jnp.where(ok, cnt, cnt_u)

    n_scored = jnp.full((1, tq), nch * tk, jnp.int32).astype(F32)
    u_thr, cnt_ge = lax.fori_loop(0, 32, radix_step, (jnp.zeros((1, tq), jnp.int32), n_scored))
    thr = _key_to_float(u_thr)

    tie = (cnt_ge > kf) & (thr > -jnp.inf)
    any_tie = jnp.sum(jnp.where(tie, 1.0, 0.0)) > 0.0

    @pl.when(any_tie)
    def _():
        cnt_gt = count(lambda s, c: s > thr)
        need_ref[...] = jnp.broadcast_to(kf - cnt_gt, need_ref.shape)
        run_ref[...] = jnp.zeros_like(run_ref)

    m_ref[...] = jnp.full(m_ref.shape, NEG_BIG, F32)
    l_ref[...] = jnp.zeros_like(l_ref)
    acc_ref[...] = jnp.zeros_like(acc_ref)
    hpg = N_HEADS // N_KV_HEADS

    rb_rows = ATT_RB
    nrb = tk // rb_rows
    pv_rows = 2 * rb_rows

    def attn_chunk(c, carry):
        def rows(rb, n=rb_rows):
            return pl.ds(pl.multiple_of(c * tk + rb * rb_rows, rb_rows), n)

        def blk(rb, n=rb_rows):
            return slice(rb * rb_rows, rb * rb_rows + n)

        def fold(x, op):
            return op(x.reshape(rb_rows // SUBLANES, SUBLANES, tq), axis=0)

        def qk_block(h, rb, bias):
            g = h // hpg
            kc = k_ref[rows(rb), g * HEAD_DIM:(g + 1) * HEAD_DIM]
            return _dot_nt(kc, q_ref[:, h * HEAD_DIM:(h + 1) * HEAD_DIM]) + bias

        def pv_half(h, slot, half, alpha):
            g = h // hpg
            cols = pl.ds(pl.multiple_of(c * tk + half * pv_rows, pv_rows), pv_rows)
            vtc = vt_ref[g * HEAD_DIM:(g + 1) * HEAD_DIM, cols]
            t = _bdot(vtc, p_ref[slot, blk(half * 2, pv_rows), :])
            if half == 0:
                acc_ref[h] = alpha * acc_ref[h] + t
            else:
                acc_ref[h] = acc_ref[h] + t

        def causal(rb):
            kp = c * tk + rb * rb_rows + lax.broadcasted_iota(jnp.int32, (rb_rows, tq), 0)
            return kp <= tpos

        @pl.when(jnp.logical_not(any_tie))
        def _():
            for rb in range(nrb):
                sel = causal(rb) & (sc_ref[rows(rb), :] >= thr)
                bias_ref[blk(rb), :] = jnp.where(sel, 0.0, NEG_BIG)

        @pl.when(any_tie)
        def _():
            need = need_ref[0:1, :]
            run = run_ref[0:1, :]
            for rb in range(nrb):
                s = sc_ref[rows(rb), :]
                eq = s == thr
                eqf = jnp.where(eq, 1.0, 0.0)
                incl = _bdot(tri_ref[...], eqf.astype(BF16))
                before = run + (incl - eqf)
                sel = causal(rb) & ((s > thr) | (eq & (before < need)))
                bias_ref[blk(rb), :] = jnp.where(sel, 0.0, NEG_BIG)
                run = run + incl[rb_rows - 1:rb_rows, :]
            run_ref[...] = jnp.broadcast_to(run, run_ref.shape)

        cm = None
        for rb in range(nrb):
            lg = qk_block(0, rb, bias_ref[blk(rb), :])
            lg_ref[0, blk(rb), :] = lg
            part = fold(lg, jnp.max)
            cm = part if cm is None else jnp.maximum(cm, part)

        alpha_prev = None
        for h in range(N_HEADS):
            cur = h % 2
            m_prev = m_ref[h]
            m_new = jnp.maximum(m_prev, jnp.max(cm, axis=0, keepdims=True))
            alpha = jnp.exp2(m_prev - m_new)
            m_ref[h] = m_new
            cm = None
            lsum = None
            for rb in range(nrb):
                if h + 1 < N_HEADS:
                    lg = qk_block(h + 1, rb, bias_ref[blk(rb), :])
                    lg_ref[1 - cur, blk(rb), :] = lg
                    part = fold(lg, jnp.max)
                    cm = part if cm is None else jnp.maximum(cm, part)
                pr = jnp.exp2(lg_ref[cur, blk(rb), :] - m_new)
                part = fold(pr, jnp.sum)
                lsum = part if lsum is None else lsum + part
                p_ref[cur, blk(rb), :] = pr.astype(BF16)
                if h > 0 and rb % 2 == 0 and rb // 2 < tk // pv_rows:
                    pv_half(h - 1, 1 - cur, rb // 2, alpha_prev)
            l_ref[h] = alpha * l_ref[h] + lsum
            alpha_prev = alpha
        last = N_HEADS - 1
        for half in range(tk // pv_rows):
            pv_half(last, last % 2, half, alpha_prev)
        return carry

    lax.fori_loop(0, nch, attn_chunk, 0)

    for h in range(N_HEADS):
        out_t = acc_ref[h] / jnp.sum(l_ref[h], axis=0, keepdims=True)
        o_ref[:, h * HEAD_DIM:(h + 1) * HEAD_DIM] = out_t.T.astype(BF16)


def _attention(q, qi, wit, ki, k, vt, batch, seq):
    n = q.shape[0]
    tq, tk = ATT_TQ, ATT_TK
    nq = seq // tq
    top_k = min(INDEX_TOPK_MAX, seq // 4)
    kvw = N_KV_HEADS * HEAD_DIM
    qtile = lambda w: pl.BlockSpec((tq, w), lambda b, j: (b * nq + j, 0))
    whole = lambda w: pl.BlockSpec((seq, w), lambda b, j: (b, 0))
    return pl.pallas_call(
        functools.partial(_attn_body, seq=seq, top_k=top_k),
        out_shape=jax.ShapeDtypeStruct((n, N_HEADS * HEAD_DIM), BF16),
        grid=(batch, nq),
        in_specs=[qtile(N_HEADS * HEAD_DIM), qtile(N_IDX_HEADS * IDX_DIM),
                  pl.BlockSpec((N_IDX_HEADS, tq), lambda b, j: (0, b * nq + j)),
                  whole(IDX_DIM), whole(kvw),
                  pl.BlockSpec((kvw, seq), lambda b, j: (0, b)),
                  pl.BlockSpec((ATT_RB, ATT_RB), lambda b, j: (0, 0))],
        out_specs=qtile(N_HEADS * HEAD_DIM),
        scratch_shapes=[
            pltpu.VMEM((seq, tq), F32),
            pltpu.VMEM((N_HEADS, 1, tq), F32),
            pltpu.VMEM((N_HEADS, SUBLANES, tq), F32),
            pltpu.VMEM((N_HEADS, HEAD_DIM, tq), F32),
            pltpu.VMEM((SUBLANES, tq), F32),
            pltpu.VMEM((SUBLANES, tq), F32),
            pltpu.VMEM((tk, tq), F32),
            pltpu.VMEM((2, tk, tq), F32),
            pltpu.VMEM((2, tk, tq), BF16),
        ],
        compiler_params=_cparams(("arbitrary", "arbitrary")),
        name="dsa_attention",
    )(q, qi, wit, ki, k, vt, jnp.tril(jnp.ones((ATT_RB, ATT_RB), BF16)))


def _merge_body(x_ref, g_ref, wg_ref, ya_ref, yb_ref, oc_ref, wao_ref, wo_ref, o_ref):
    x = x_ref[...]
    u = _rmsnorm_f32(x, g_ref[...]).astype(BF16)
    gt = jax.nn.sigmoid(_bdot(u, wg_ref[...]))
    yc = _bdot(oc_ref[...], wao_ref[...])
    merged = (gt[:, :D_MODEL] * ya_ref[...] + gt[:, D_MODEL:2 * D_MODEL] * yb_ref[...]
              + gt[:, 2 * D_MODEL:] * yc)
    o_ref[...] = x + _bdot(merged.astype(BF16), wo_ref[...])


def _merge(x, gain, w_gates, ya, yb, oc, w_attn_out, w_o):
    n = x.shape[0]
    tm = MERGE_TM
    full = lambda shape: pl.BlockSpec(shape, lambda i: (0,) * len(shape))
    tile = pl.BlockSpec((tm, D_MODEL), lambda i: (i, 0))
    return pl.pallas_call(
        _merge_body,
        out_shape=jax.ShapeDtypeStruct((n, D_MODEL), F32),
        grid=(n // tm,),
        in_specs=[tile, full((1, D_MODEL)), full((D_MODEL, N_BRANCHES * D_MODEL)), tile, tile, tile,
                  full((N_HEADS * HEAD_DIM, D_MODEL)), full((D_MODEL, D_MODEL))],
        out_specs=tile,
        compiler_params=_cparams(("arbitrary",)),
        name="merge",
    )(x, gain, w_gates, ya, yb, oc, w_attn_out, w_o)


def _block_diag_groups(w):
    per = GATE_GROUP // RNN_BLOCK_W
    w4 = w.reshape(N_GATE_GROUPS, per, RNN_BLOCK_W, RNN_BLOCK_W)
    eye = jnp.eye(per, dtype=w.dtype)
    return jnp.einsum("gnij,nm->gnimj", w4, eye).reshape(N_GATE_GROUPS, GATE_GROUP, GATE_GROUP)


def kernel(x, positions, ffn1_norm, ffn1_w_gate_up, ffn1_w_down, mix_norm, w_in, rnn_conv_w, rnn_conv_b,
           rnn_gate_a_w, rnn_gate_a_b, rnn_gate_x_w, rnn_gate_x_b, rnn_lambda, rnn_w_out, sconv_w,
           sconv_w_out, attn_w_out, w_o, ffn2_norm, ffn2_w_gate_up, ffn2_w_down, final_norm):
    batch, seq, d = x.shape
    depth = w_in.shape[0]
    n = batch * seq
    xf = x.reshape(n, d)
    row = lambda v: v.reshape(1, -1)

    tabs = _rope_tables(positions.reshape(n, 1))

    o_rnn = 0
    o_bch = o_rnn + 2 * D_RNN
    o_q = o_bch + 3 * D_SCONV
    o_k = o_q + N_HEADS * HEAD_DIM
    o_v = o_k + N_KV_HEADS * HEAD_DIM
    o_qi = o_v + N_KV_HEADS * HEAD_DIM
    o_ki = o_qi + N_IDX_HEADS * IDX_DIM
    o_wi = o_ki + IDX_DIM
    o_gates = o_wi + N_IDX_HEADS

    for l in range(depth):
        win = w_in[l]
        w_rnn = win[:, o_rnn:o_bch].astype(BF16)
        w_bch = win[:, o_bch:o_q].astype(BF16)
        w_qkv = jnp.concatenate([win[:, o_q:o_v], win[:, o_qi:o_wi],
                                 jnp.zeros((d, LANES - IDX_DIM), win.dtype)], axis=1).astype(BF16)
        w_t = jnp.concatenate([win[:, o_v:o_qi], win[:, o_wi:o_gates]], axis=1).T.astype(BF16)
        w_gates = win[:, o_gates:].astype(BF16)
        wbd = jnp.concatenate([_block_diag_groups(rnn_gate_a_w[l]), _block_diag_groups(rnn_gate_x_w[l])],
                              axis=-1).astype(BF16)

        xf = _ffn(xf, row(ffn1_norm[l]), ffn1_w_gate_up[l].astype(BF16), ffn1_w_down[l].astype(BF16),
                  row(final_norm), False)

        gain = row(mix_norm[l])
        ya = _rnn_branch(xf, gain, w_rnn, rnn_conv_w[l], row(rnn_conv_b[l]), wbd, row(rnn_gate_a_b[l]),
                         row(rnn_gate_x_b[l]), row(rnn_lambda[l]), rnn_w_out[l].astype(BF16), seq)
        yb = _sconv_branch(xf, gain, w_bch, sconv_w[l], sconv_w_out[l].astype(BF16), seq)
        q, k, qi, ki, vt, wit = _qkv_proj(xf, gain, w_qkv, w_t, tabs)
        oc = _attention(q, qi, wit, ki, k, vt, batch, seq)
        xf = _merge(xf, gain, w_gates, ya, yb, oc, attn_w_out[l].astype(BF16), w_o[l].astype(BF16))

        xf = _ffn(xf, row(ffn2_norm[l]), ffn2_w_gate_up[l].astype(BF16), ffn2_w_down[l].astype(BF16),
                  row(final_norm), l == depth - 1)

    return xf.reshape(batch, seq, d)
```

```python
import functools
import math

import jax
import jax.numpy as jnp
import numpy as np
from jax import lax
from jax.experimental import pallas as pl
from jax.experimental.pallas import tpu as pltpu

F32 = jnp.float32
BF16 = jnp.bfloat16

D_MODEL = 1024
D_FF = 2816
D_RNN = 1024
RNN_BLOCKS = 16
RNN_BLOCK_W = D_RNN // RNN_BLOCKS
RNN_CONV_W = 4
RG_LRU_C = 8.0
D_SCONV = 1024
SCONV_W = 3
N_HEADS = 8
N_KV_HEADS = 2
HEAD_DIM = 128
N_IDX_HEADS = 8
IDX_DIM = 64
INDEX_TOPK_MAX = 256
ROPE_THETA = 500000.0
ROPE_FRACTION_DEN = 4
N_BRANCHES = 3
RMS_EPS = 1e-6

LANES = 128
SUBLANES = 8
VMEM_LIMIT = 56 * 1024 * 1024

FFN_TM = 1024
FFN_TF = 256
RNN_TM = 256
SCONV_TM = 512
QKV_TM = 512
MERGE_TM = 512
TAB_TM = 1024
ATT_TQ = 256
ATT_TK = 512
ATT_RB = 128

GATE_GROUP = 256
N_GATE_GROUPS = D_RNN // GATE_GROUP

NEG_BIG = -1e30
LOGIT_SCALE = (HEAD_DIM ** -0.5) * math.log2(math.e)
KEY_NEG_INF = np.int32(np.uint32(0x807FFFFF).view(np.int32))


def _cparams(sem, flags=None):
    return pltpu.CompilerParams(dimension_semantics=sem, vmem_limit_bytes=VMEM_LIMIT, flags=flags)


def _layer_spec(l, shape, col=0):
    zeros = (0,) * (len(shape) - 1)
    return pl.BlockSpec((None,) + tuple(shape), lambda *_: (l,) + zeros + (col,))


def _rmsnorm_f32(x, g):
    var = jnp.mean(x * x, axis=-1, keepdims=True)
    return (x * lax.rsqrt(var + RMS_EPS)) * g


def _bdot(a, b):
    return jnp.dot(a, b, preferred_element_type=F32)


def _dot_nt(a, b):
    return lax.dot_general(a, b, (((1,), (1,)), ((), ())), preferred_element_type=F32)


def _ffn_body(x_ref, g_ref, wg_ref, wu_ref, wd_ref, fn_ref, o_ref, xn_ref, acc_ref, *, n_ff, final_norm):
    j = pl.program_id(1)

    @pl.when(j == 0)
    def _():
        xn_ref[...] = _rmsnorm_f32(x_ref[...], g_ref[...]).astype(BF16)
        acc_ref[...] = jnp.zeros_like(acc_ref)

    xn = xn_ref[...]
    g = _bdot(xn, wg_ref[...])
    u = _bdot(xn, wu_ref[...])
    a = (g * jax.nn.sigmoid(g) * u).astype(BF16)
    acc_ref[...] += _bdot(a, wd_ref[...])

    @pl.when(j == n_ff - 1)
    def _():
        y = x_ref[...] + 0.5 * acc_ref[...]
        if final_norm:
            y = _rmsnorm_f32(y, fn_ref[...])
        o_ref[...] = y


def _ffn(x, l, gain, w_gu, w_d, fnorm, final_norm):
    n = x.shape[0]
    n_ff = D_FF // FFN_TF
    grid = (n // FFN_TM, n_ff)
    return pl.pallas_call(
        functools.partial(_ffn_body, n_ff=n_ff, final_norm=final_norm),
        out_shape=jax.ShapeDtypeStruct((n, D_MODEL), F32),
        grid=grid,
        in_specs=[
            pl.BlockSpec((FFN_TM, D_MODEL), lambda i, j: (i, 0)),
            pl.BlockSpec((None, 1, D_MODEL), lambda i, j: (l, 0, 0)),
            pl.BlockSpec((None, D_MODEL, FFN_TF), lambda i, j: (l, 0, j)),
            pl.BlockSpec((None, D_MODEL, FFN_TF), lambda i, j: (l, 0, j + n_ff)),
            pl.BlockSpec((None, FFN_TF, D_MODEL), lambda i, j: (l, j, 0)),
            pl.BlockSpec((1, D_MODEL), lambda i, j: (0, 0)),
        ],
        out_specs=pl.BlockSpec((FFN_TM, D_MODEL), lambda i, j: (i, 0)),
        scratch_shapes=[pltpu.VMEM((FFN_TM, D_MODEL), BF16), pltpu.VMEM((FFN_TM, D_MODEL), F32)],
        compiler_params=_cparams(("arbitrary", "arbitrary")),
        name="ffn",
    )(x, gain, w_gu, w_gu, w_d, fnorm)


def _causal_dwconv_tile(x, prev_rows, w_ref, width):
    tm, ch = x.shape
    groups = tm // SUBLANES
    x3 = x.reshape(groups, SUBLANES, ch)
    sub = lax.broadcasted_iota(jnp.int32, x3.shape, 1)
    out = x3 * w_ref[width - 1:width, :]
    for lag in range(1, width):
        cur = pltpu.roll(x3, lag, 1)
        head = pltpu.roll(prev_rows.reshape(1, SUBLANES, ch), lag, 1)
        prev = jnp.concatenate([head, cur[:groups - 1]], axis=0)
        out = out + jnp.where(sub >= lag, cur, prev) * w_ref[width - 1 - lag:width - lag, :]
    return out.reshape(tm, ch)


def _rnn_body(x_ref, g_ref, w_ref, cw_ref, cb_ref, wbd_ref, ba_ref, bx_ref, lam_ref, wout_ref,
              o_ref, xbuf_ref, h_ref, *, tiles_per_seq):
    tm = x_ref.shape[0]
    i = pl.program_id(0)

    @pl.when(i % tiles_per_seq == 0)
    def _():
        xbuf_ref[...] = jnp.zeros_like(xbuf_ref)
        h_ref[...] = jnp.zeros_like(h_ref)

    u = _rmsnorm_f32(x_ref[...], g_ref[...]).astype(BF16)
    p = _bdot(u, w_ref[...])
    rx = p[:, :D_RNN]
    rg = p[:, D_RNN:]

    xa = cb_ref[...] + _causal_dwconv_tile(rx, xbuf_ref[...], cw_ref, RNN_CONV_W)
    xbuf_ref[...] = rx[tm - SUBLANES:, :]

    xab = xa.astype(BF16)
    r_parts, i_parts = [], []
    for jg in range(N_GATE_GROUPS):
        gj = _bdot(xab[:, jg * GATE_GROUP:(jg + 1) * GATE_GROUP], wbd_ref[jg])
        r_parts.append(gj[:, :GATE_GROUP])
        i_parts.append(gj[:, GATE_GROUP:])
    r = jax.nn.sigmoid(jnp.concatenate(r_parts, axis=1) + ba_ref[...])
    ig = jax.nn.sigmoid(jnp.concatenate(i_parts, axis=1) + bx_ref[...])

    nl = -lam_ref[...]
    sp = jnp.maximum(nl, 0.0) + jnp.log1p(jnp.exp(-jnp.abs(nl)))
    log_a = (-RG_LRU_C) * r * sp
    a = jnp.exp(log_a)
    one_m_a2 = -jnp.tanh(log_a) * (a * a + 1.0)
    in_scale = jnp.where(one_m_a2 > 0.0, one_m_a2 * lax.rsqrt(one_m_a2), 0.0)
    b = in_scale * (ig * xa)

    groups = tm // SUBLANES
    a3 = a.reshape(groups, SUBLANES, D_RNN)
    b3 = b.reshape(groups, SUBLANES, D_RNN)
    sub = lax.broadcasted_iota(jnp.int32, a3.shape, 1)
    d = 1
    while d < SUBLANES:
        keep = sub >= d
        a_s = jnp.where(keep, pltpu.roll(a3, d, 1), 1.0)
        b_s = jnp.where(keep, pltpu.roll(b3, d, 1), 0.0)
        b3 = a3 * b_s + b3
        a3 = a3 * a_s
        d *= 2
    carry = h_ref[...]
    h_groups = []
    for gi in range(groups):
        hg = b3[gi] + a3[gi] * carry
        h_groups.append(hg)
        carry = hg[SUBLANES - 1:SUBLANES, :]
    h_ref[...] = carry
    h = jnp.concatenate(h_groups, axis=0)

    gl = 0.5 * rg * (1.0 + jnp.tanh(math.sqrt(2.0 / math.pi) * (rg + 0.044715 * (rg * rg * rg))))
    o_ref[...] = _bdot((h * gl).astype(BF16), wout_ref[...])


def _rnn_branch(x, l, gain, w_in, conv_w, conv_b, wbd, ba, bx, lam, w_out, seq):
    n = x.shape[0]
    tm = RNN_TM
    return pl.pallas_call(
        functools.partial(_rnn_body, tiles_per_seq=seq // tm),
        out_shape=jax.ShapeDtypeStruct((n, D_MODEL), F32),
        grid=(n // tm,),
        in_specs=[
            pl.BlockSpec((tm, D_MODEL), lambda i: (i, 0)),
            _layer_spec(l, (1, D_MODEL)),
            _layer_spec(l, (D_MODEL, 2 * D_RNN)),
            _layer_spec(l, (RNN_CONV_W, D_RNN)),
            _layer_spec(l, (1, D_RNN)),
            _layer_spec(l, (N_GATE_GROUPS, GATE_GROUP, 2 * GATE_GROUP)),
            _layer_spec(l, (1, D_RNN)),
            _layer_spec(l, (1, D_RNN)),
            _layer_spec(l, (1, D_RNN)),
            _layer_spec(l, (D_RNN, D_MODEL)),
        ],
        out_specs=pl.BlockSpec((tm, D_MODEL), lambda i: (i, 0)),
        scratch_shapes=[pltpu.VMEM((SUBLANES, D_RNN), F32), pltpu.VMEM((1, D_RNN), F32)],
        compiler_params=_cparams(("arbitrary",)),
        name="rnn_branch",
    )(x, gain, w_in, conv_w, conv_b, wbd, ba, bx, lam, w_out)


def _sconv_body(x_ref, g_ref, wb_ref, wc_ref, wh_ref, sw_ref, wout_ref, o_ref, zbuf_ref, *, tiles_per_seq):
    tm = x_ref.shape[0]
    i = pl.program_id(0)

    @pl.when(i % tiles_per_seq == 0)
    def _():
        zbuf_ref[...] = jnp.zeros_like(zbuf_ref)

    u = _rmsnorm_f32(x_ref[...], g_ref[...]).astype(BF16)
    c_b = _bdot(u, wb_ref[...])
    z = _bdot(u, wc_ref[...]) * _bdot(u, wh_ref[...])
    cv = _causal_dwconv_tile(z, zbuf_ref[...], sw_ref, SCONV_W)
    zbuf_ref[...] = z[tm - SUBLANES:, :]
    o_ref[...] = _bdot((c_b * cv).astype(BF16), wout_ref[...])


def _sconv_branch(x, l, gain, w_in, sw, w_out, seq):
    n = x.shape[0]
    tm = SCONV_TM
    col0 = 2 * D_RNN // D_SCONV
    return pl.pallas_call(
        functools.partial(_sconv_body, tiles_per_seq=seq // tm),
        out_shape=jax.ShapeDtypeStruct((n, D_MODEL), F32),
        grid=(n // tm,),
        in_specs=[
            pl.BlockSpec((tm, D_MODEL), lambda i: (i, 0)),
            _layer_spec(l, (1, D_MODEL)),
            _layer_spec(l, (D_MODEL, D_SCONV), col0),
            _layer_spec(l, (D_MODEL, D_SCONV), col0 + 1),
            _layer_spec(l, (D_MODEL, D_SCONV), col0 + 2),
            _layer_spec(l, (SCONV_W, D_SCONV)),
            _layer_spec(l, (D_SCONV, D_MODEL)),
        ],
        out_specs=pl.BlockSpec((tm, D_MODEL), lambda i: (i, 0)),
        scratch_shapes=[pltpu.VMEM((SUBLANES, D_SCONV), F32)],
        compiler_params=_cparams(("arbitrary",)),
        name="sconv_branch",
    )(x, gain, w_in, w_in, w_in, sw, w_out)


def _tab_body(pos_ref, f128_ref, s128_ref, f64_ref, s64_ref, c128_ref, sg128_ref, c64_ref, sg64_ref):
    posf = pos_ref[...].astype(F32)
    for f_ref, s_ref, c_out, s_out in ((f128_ref, s128_ref, c128_ref, sg128_ref),
                                       (f64_ref, s64_ref, c64_ref, sg64_ref)):
        sgn = s_ref[...]
        ang = posf * f_ref[...]
        c_out[...] = jnp.where(sgn != 0.0, jnp.cos(ang), 1.0)
        s_out[...] = jnp.sin(ang) * sgn


def _rope_lane_rows(period, rot):
    half = rot // 2
    inv_freq = ROPE_THETA ** (-jnp.arange(0, rot, 2, dtype=F32) / rot)
    lane = np.arange(LANES) % period
    f_row = jnp.where(lane < rot, inv_freq[lane % half], 0.0).astype(F32)
    s_row = np.where(lane < half, -1.0, np.where(lane < rot, 1.0, 0.0)).astype(np.float32)
    return f_row.reshape(1, LANES), jnp.asarray(s_row).reshape(1, LANES)


def _rope_tables(pos_col):
    n = pos_col.shape[0]
    f128, s128 = _rope_lane_rows(HEAD_DIM, HEAD_DIM // ROPE_FRACTION_DEN)
    f64, s64 = _rope_lane_rows(IDX_DIM, IDX_DIM // ROPE_FRACTION_DEN)
    row = pl.BlockSpec((1, LANES), lambda i: (0, 0))
    tab = pl.BlockSpec((TAB_TM, LANES), lambda i: (i, 0))
    return pl.pallas_call(
        _tab_body,
        out_shape=[jax.ShapeDtypeStruct((n, LANES), F32)] * 4,
        grid=(n // TAB_TM,),
        in_specs=[pl.BlockSpec((TAB_TM, 1), lambda i: (i, 0)), row, row, row, row],
        out_specs=[tab] * 4,
        compiler_params=_cparams(("arbitrary",)),
        name="rope_tables",
    )(pos_col, f128, s128, f64, s64)


def _rope_apply(xh, c, sg, first_half, half):
    partner = jnp.where(first_half, pltpu.roll(xh, LANES - half, 1), pltpu.roll(xh, half, 1))
    return xh * c + partner * sg


def _qkv_body(x_ref, g_ref, wq_ref, wk_ref, wqi_ref, wki_ref, wt_ref, c128_ref, s128_ref, c64_ref, s64_ref,
              q_ref, k_ref, qi_ref, ki_ref, vt_ref, wit_ref):
    tm = x_ref.shape[0]
    u = _rmsnorm_f32(x_ref[...], g_ref[...]).astype(BF16)
    p = jnp.concatenate([_bdot(u, w[...]) for w in (wq_ref, wk_ref, wqi_ref, wki_ref)], axis=1)
    pt = _dot_nt(wt_ref[...], u)
    kvw = N_KV_HEADS * HEAD_DIM
    vt_ref[...] = pt[:kvw, :].astype(BF16)
    wit_ref[...] = pt[kvw:kvw + N_IDX_HEADS, :]
    lane = lax.broadcasted_iota(jnp.int32, (tm, LANES), 1)
    h128 = HEAD_DIM // ROPE_FRACTION_DEN // 2
    h64 = IDX_DIM // ROPE_FRACTION_DEN // 2
    first128 = lane < h128
    first64 = (lane % IDX_DIM) < h64
    c128, s128 = c128_ref[...], s128_ref[...]
    c64, s64 = c64_ref[...], s64_ref[...]

    off = 0
    for h in range(N_HEADS):
        blk = p[:, off + h * LANES: off + (h + 1) * LANES]
        roped = _rope_apply(blk, c128, s128, first128, h128)
        q_ref[:, h * LANES:(h + 1) * LANES] = (roped * LOGIT_SCALE).astype(BF16)
    off += N_HEADS * HEAD_DIM
    for h in range(N_KV_HEADS):
        blk = p[:, off + h * LANES: off + (h + 1) * LANES]
        k_ref[:, h * LANES:(h + 1) * LANES] = _rope_apply(blk, c128, s128, first128, h128).astype(BF16)
    off += N_KV_HEADS * HEAD_DIM
    for h in range(N_IDX_HEADS * IDX_DIM // LANES):
        blk = p[:, off + h * LANES: off + (h + 1) * LANES]
        qi_ref[:, h * LANES:(h + 1) * LANES] = _rope_apply(blk, c64, s64, first64, h64).astype(BF16)
    off += N_IDX_HEADS * IDX_DIM
    blk = p[:, off: off + LANES]
    ki_ref[...] = _rope_apply(blk, c64, s64, first64, h64)[:, :IDX_DIM].astype(BF16)


def _qkv_proj(x, l, gain, w_in, w_t, tabs):
    n = x.shape[0]
    tm = QKV_TM
    qw, kw, qiw = N_HEADS * HEAD_DIM, N_KV_HEADS * HEAD_DIM, N_IDX_HEADS * IDX_DIM
    o_q = 2 * D_RNN + 3 * D_SCONV
    o_k = o_q + qw
    o_qi = o_k + 2 * kw
    o_ki = o_qi + qiw
    tile = lambda w: pl.BlockSpec((tm, w), lambda i: (i, 0))
    ttile = lambda r: pl.BlockSpec((r, tm), lambda i: (0, i))
    kvw = N_KV_HEADS * HEAD_DIM
    return pl.pallas_call(
        _qkv_body,
        out_shape=[
            jax.ShapeDtypeStruct((n, N_HEADS * HEAD_DIM), BF16),
            jax.ShapeDtypeStruct((n, kvw), BF16),
            jax.ShapeDtypeStruct((n, N_IDX_HEADS * IDX_DIM), BF16),
            jax.ShapeDtypeStruct((n, IDX_DIM), BF16),
            jax.ShapeDtypeStruct((kvw, n), BF16),
            jax.ShapeDtypeStruct((N_IDX_HEADS, n), F32),
        ],
        grid=(n // tm,),
        in_specs=[tile(D_MODEL), _layer_spec(l, (1, D_MODEL)),
                  _layer_spec(l, (D_MODEL, qw), o_q // qw),
                  _layer_spec(l, (D_MODEL, kw), o_k // kw),
                  _layer_spec(l, (D_MODEL, qiw), o_qi // qiw),
                  _layer_spec(l, (D_MODEL, LANES), o_ki // LANES),
                  _layer_spec(l, w_t.shape[1:]),
                  tile(LANES), tile(LANES), tile(LANES), tile(LANES)],
        out_specs=[tile(N_HEADS * HEAD_DIM), tile(kvw), tile(N_IDX_HEADS * IDX_DIM), tile(IDX_DIM),
                   ttile(kvw), ttile(N_IDX_HEADS)],
        compiler_params=_cparams(("arbitrary",)),
        name="qkv_proj",
    )(x, gain, w_in, w_in, w_in, w_in, w_t, *tabs)


def _key_to_float(u):
    key = u + KEY_NEG_INF
    bits = jnp.where(key < 0, key ^ jnp.int32(0x7FFFFFFF), key)
    return lax.bitcast_convert_type(bits, F32)


def _attn_body(q_ref, qi_ref, wit_ref, ki_ref, k_ref, vt_ref, tri_ref, o_ref,
               sc_ref, m_ref, l_ref, acc_ref, need_ref, run_ref, bias_ref, lg_ref, p_ref, *, seq, top_k):
    tq, tk = ATT_TQ, ATT_TK
    j = pl.program_id(1)
    q0 = j * tq
    nch = (q0 + tq - 1) // tk + 1
    tpos = q0 + lax.broadcasted_iota(jnp.int32, (1, tq), 1)
    w_scale = (N_IDX_HEADS ** -0.5) * (IDX_DIM ** -0.5)

    def chunk_rows(c):
        return pl.ds(pl.multiple_of(c * tk, tk), tk)

    def key_pos(c):
        return c * tk + lax.broadcasted_iota(jnp.int32, (tk, tq), 0)

    def fold_rows(x):
        groups = tk // SUBLANES
        side = int(math.isqrt(groups))
        x4 = x.reshape(groups // side, side, SUBLANES, tq)
        return jnp.sum(jnp.sum(x4, axis=1), axis=0)

    wsc = wit_ref[...] * w_scale

    def score_chunk(c, carry):
        for rb in range(tk // ATT_RB):
            r0 = pl.multiple_of(c * tk + rb * ATT_RB, ATT_RB)
            kic = ki_ref[pl.ds(r0, ATT_RB), :]
            acc = jnp.zeros((ATT_RB, tq), F32)
            for h in range(N_IDX_HEADS):
                rel = _dot_nt(kic, qi_ref[:, h * IDX_DIM:(h + 1) * IDX_DIM])
                acc = acc + wsc[h:h + 1, :] * jnp.maximum(rel, 0.0)
            kp = r0 + lax.broadcasted_iota(jnp.int32, (ATT_RB, tq), 0)
            sc_ref[pl.ds(r0, ATT_RB), :] = jnp.where(kp <= tpos, acc, -jnp.inf)
        return carry

    lax.fori_loop(0, nch, score_chunk, 0)

    def count(pred):
        def body(c, acc):
            return acc + fold_rows(jnp.where(pred(sc_ref[chunk_rows(c), :], c), 1.0, 0.0))
        acc = lax.fori_loop(0, nch, body, jnp.zeros((SUBLANES, tq), F32))
        return jnp.sum(acc, axis=0, keepdims=True)

    kf = float(top_k)

    def radix_step(it, carry):
        u, cnt_u = carry
        cand = u | lax.shift_left(jnp.int32(1), 31 - it)
        cf = _key_to_float(cand)
        cnt = count(lambda s, c: s >= cf)
        ok = cnt >= kf
        return jnp.where(ok, cand, u), jnp.where(ok, cnt, cnt_u)

    n_scored = jnp.full((1, tq), nch * tk, jnp.int32).astype(F32)
    u_thr, cnt_ge = lax.fori_loop(0, 32, radix_step, (jnp.zeros((1, tq), jnp.int32), n_scored))
    thr = _key_to_float(u_thr)

    tie = (cnt_ge > kf) & (thr > -jnp.inf)
    any_tie = jnp.sum(jnp.where(tie, 1.0, 0.0)) > 0.0

    @pl.when(any_tie)
    def _():
        cnt_gt = count(lambda s, c: s > thr)
        need_ref[...] = jnp.broadcast_to(kf - cnt_gt, need_ref.shape)
        run_ref[...] = jnp.zeros_like(run_ref)

    m_ref[...] = jnp.full(m_ref.shape, NEG_BIG, F32)
    l_ref[...] = jnp.zeros_like(l_ref)
    acc_ref[...] = jnp.zeros_like(acc_ref)
    hpg = N_HEADS // N_KV_HEADS

    rb_rows = ATT_RB
    nrb = tk // rb_rows
    pv_rows = 2 * rb_rows

    def attn_chunk(c, carry):
        def rows(rb, n=rb_rows):
            return pl.ds(pl.multiple_of(c * tk + rb * rb_rows, rb_rows), n)

        def blk(rb, n=rb_rows):
            return slice(rb * rb_rows, rb * rb_rows + n)

        def fold(x, op):
            return op(x.reshape(rb_rows // SUBLANES, SUBLANES, tq), axis=0)

        def qk_block(h, rb, bias):
            g = h // hpg
            kc = k_ref[rows(rb), g * HEAD_DIM:(g + 1) * HEAD_DIM]
            return _dot_nt(kc, q_ref[:, h * HEAD_DIM:(h + 1) * HEAD_DIM]) + bias

        def pv_half(h, slot, half, alpha):
            g = h // hpg
            cols = pl.ds(pl.multiple_of(c * tk + half * pv_rows, pv_rows), pv_rows)
            vtc = vt_ref[g * HEAD_DIM:(g + 1) * HEAD_DIM, cols]
            t = _bdot(vtc, p_ref[slot, blk(half * 2, pv_rows), :])
            if half == 0:
                acc_ref[h] = alpha * acc_ref[h] + t
            else:
                acc_ref[h] = acc_ref[h] + t

        def causal(rb):
            kp = c * tk + rb * rb_rows + lax.broadcasted_iota(jnp.int32, (rb_rows, tq), 0)
            return kp <= tpos

        @pl.when(jnp.logical_not(any_tie))
        def _():
            for rb in range(nrb):
                sel = causal(rb) & (sc_ref[rows(rb), :] >= thr)
                bias_ref[blk(rb), :] = jnp.where(sel, 0.0, NEG_BIG)

        @pl.when(any_tie)
        def _():
            need = need_ref[0:1, :]
            run = run_ref[0:1, :]
            for rb in range(nrb):
                s = sc_ref[rows(rb), :]
                eq = s == thr
                eqf = jnp.where(eq, 1.0, 0.0)
                incl = _bdot(tri_ref[...], eqf.astype(BF16))
                before = run + (incl - eqf)
                sel = causal(rb) & ((s > thr) | (eq & (before < need)))
                bias_ref[blk(rb), :] = jnp.where(sel, 0.0, NEG_BIG)
                run = run + incl[rb_rows - 1:rb_rows, :]
            run_ref[...] = jnp.broadcast_to(run, run_ref.shape)

        cm = None
        for rb in range(nrb):
            lg = qk_block(0, rb, bias_ref[blk(rb), :])
            lg_ref[0, blk(rb), :] = lg
            part = fold(lg, jnp.max)
            cm = part if cm is None else jnp.maximum(cm, part)

        alpha_prev = None
        for h in range(N_HEADS):
            cur = h % 2
            m_prev = m_ref[h]
            m_new = jnp.maximum(m_prev, jnp.max(cm, axis=0, keepdims=True))
            alpha = jnp.exp2(m_prev - m_new)
            m_ref[h] = m_new
            cm = None
            lsum = None
            for rb in range(nrb):
                if h + 1 < N_HEADS:
                    lg = qk_block(h + 1, rb, bias_ref[blk(rb), :])
                    lg_ref[1 - cur, blk(rb), :] = lg
                    part = fold(lg, jnp.max)
                    cm = part if cm is None else jnp.maximum(cm, part)
                pr = jnp.exp2(lg_ref[cur, blk(rb), :] - m_new)
                part = fold(pr, jnp.sum)
                lsum = part if lsum is None else lsum + part
                p_ref[cur, blk(rb), :] = pr.astype(BF16)
                if h > 0 and rb % 2 == 0 and rb // 2 < tk // pv_rows:
                    pv_half(h - 1, 1 - cur, rb // 2, alpha_prev)
            l_ref[h] = alpha * l_ref[h] + lsum
            alpha_prev = alpha
        last = N_HEADS - 1
        for half in range(tk // pv_rows):
            pv_half(last, last % 2, half, alpha_prev)
        return carry

    lax.fori_loop(0, nch, attn_chunk, 0)

    for h in range(N_HEADS):
        out_t = acc_ref[h] / jnp.sum(l_ref[h], axis=0, keepdims=True)
        o_ref[:, h * HEAD_DIM:(h + 1) * HEAD_DIM] = out_t.T.astype(BF16)


def _attention(q, qi, wit, ki, k, vt, batch, seq):
    n = q.shape[0]
    tq, tk = ATT_TQ, ATT_TK
    nq = seq // tq
    top_k = min(INDEX_TOPK_MAX, seq // 4)
    kvw = N_KV_HEADS * HEAD_DIM
    qtile = lambda w: pl.BlockSpec((tq, w), lambda b, j: (b * nq + j, 0))
    whole = lambda w: pl.BlockSpec((seq, w), lambda b, j: (b, 0))
    return pl.pallas_call(
        functools.partial(_attn_body, seq=seq, top_k=top_k),
        out_shape=jax.ShapeDtypeStruct((n, N_HEADS * HEAD_DIM), BF16),
        grid=(batch, nq),
        in_specs=[qtile(N_HEADS * HEAD_DIM), qtile(N_IDX_HEADS * IDX_DIM),
                  pl.BlockSpec((N_IDX_HEADS, tq), lambda b, j: (0, b * nq + j)),
                  whole(IDX_DIM), whole(kvw),
                  pl.BlockSpec((kvw, seq), lambda b, j: (0, b)),
                  pl.BlockSpec((ATT_RB, ATT_RB), lambda b, j: (0, 0))],
        out_specs=qtile(N_HEADS * HEAD_DIM),
        scratch_shapes=[
            pltpu.VMEM((seq, tq), F32),
            pltpu.VMEM((N_HEADS, 1, tq), F32),
            pltpu.VMEM((N_HEADS, SUBLANES, tq), F32),
            pltpu.VMEM((N_HEADS, HEAD_DIM, tq), F32),
            pltpu.VMEM((SUBLANES, tq), F32),
            pltpu.VMEM((SUBLANES, tq), F32),
            pltpu.VMEM((tk, tq), F32),
            pltpu.VMEM((2, tk, tq), F32),
            pltpu.VMEM((2, tk, tq), BF16),
        ],
        compiler_params=_cparams(("arbitrary", "arbitrary")),
        name="dsa_attention",
    )(q, qi, wit, ki, k, vt, jnp.tril(jnp.ones((ATT_RB, ATT_RB), BF16)))


def _merge_body(x_ref, g_ref, wg_ref, ya_ref, yb_ref, oc_ref, wao_ref, wo_ref, o_ref):
    x = x_ref[...]
    u = _rmsnorm_f32(x, g_ref[...]).astype(BF16)
    gt = jax.nn.sigmoid(_bdot(u, wg_ref[...]))
    yc = _bdot(oc_ref[...], wao_ref[...])
    merged = (gt[:, :D_MODEL] * ya_ref[...] + gt[:, D_MODEL:2 * D_MODEL] * yb_ref[...]
              + gt[:, 2 * D_MODEL:] * yc)
    o_ref[...] = x + _bdot(merged.astype(BF16), wo_ref[...])


def _merge(x, l, gain, w_gates, ya, yb, oc, w_attn_out, w_o):
    n = x.shape[0]
    tm = MERGE_TM
    tile = pl.BlockSpec((tm, D_MODEL), lambda i: (i, 0))
    return pl.pallas_call(
        _merge_body,
        out_shape=jax.ShapeDtypeStruct((n, D_MODEL), F32),
        grid=(n // tm,),
        in_specs=[tile, _layer_spec(l, (1, D_MODEL)), _layer_spec(l, (D_MODEL, N_BRANCHES * D_MODEL)),
                  tile, tile, tile,
                  _layer_spec(l, (N_HEADS * HEAD_DIM, D_MODEL)), _layer_spec(l, (D_MODEL, D_MODEL))],
        out_specs=tile,
        compiler_params=_cparams(("arbitrary",)),
        name="merge",
    )(x, gain, w_gates, ya, yb, oc, w_attn_out, w_o)


def _block_diag_groups(w):
    depth = w.shape[0]
    per = GATE_GROUP // RNN_BLOCK_W
    w5 = w.reshape(depth, N_GATE_GROUPS, per, RNN_BLOCK_W, RNN_BLOCK_W)
    eye = jnp.eye(per, dtype=w.dtype)
    return jnp.einsum("lgnij,nm->lgnimj", w5, eye).reshape(depth, N_GATE_GROUPS, GATE_GROUP, GATE_GROUP)


def kernel(x, positions, ffn1_norm, ffn1_w_gate_up, ffn1_w_down, mix_norm, w_in, rnn_conv_w, rnn_conv_b,
           rnn_gate_a_w, rnn_gate_a_b, rnn_gate_x_w, rnn_gate_x_b, rnn_lambda, rnn_w_out, sconv_w,
           sconv_w_out, attn_w_out, w_o, ffn2_norm, ffn2_w_gate_up, ffn2_w_down, final_norm):
    batch, seq, d = x.shape
    depth = w_in.shape[0]
    n = batch * seq
    xf = x.reshape(n, d)
    rows = lambda v: v.reshape(depth, 1, -1)
    bf = lambda w: w.astype(BF16)

    tabs = _rope_tables(positions.reshape(n, 1))

    o_v = 2 * D_RNN + 3 * D_SCONV + (N_HEADS + N_KV_HEADS) * HEAD_DIM
    o_qi = o_v + N_KV_HEADS * HEAD_DIM
    o_wi = o_qi + N_IDX_HEADS * IDX_DIM + IDX_DIM
    o_gates = o_wi + N_IDX_HEADS

    w_in_b = bf(w_in)
    w_gates = bf(w_in[:, :, o_gates:])
    w_t = bf(jnp.swapaxes(jnp.concatenate([w_in[:, :, o_v:o_qi], w_in[:, :, o_wi:o_gates]], axis=2), 1, 2))
    wbd = bf(jnp.concatenate([_block_diag_groups(rnn_gate_a_w), _block_diag_groups(rnn_gate_x_w)], axis=-1))
    ffn1_gu, ffn1_d, ffn2_gu, ffn2_d = bf(ffn1_w_gate_up), bf(ffn1_w_down), bf(ffn2_w_gate_up), bf(ffn2_w_down)
    rnn_wo, sconv_wo, attn_wo, w_o_b = bf(rnn_w_out), bf(sconv_w_out), bf(attn_w_out), bf(w_o)
    g_ffn1, g_mix, g_ffn2 = rows(ffn1_norm), rows(mix_norm), rows(ffn2_norm)
    conv_b, gate_a_b, gate_x_b, lam = rows(rnn_conv_b), rows(rnn_gate_a_b), rows(rnn_gate_x_b), rows(rnn_lambda)
    g_final = final_norm.reshape(1, -1)

    for l in range(depth):
        xf = _ffn(xf, l, g_ffn1, ffn1_gu, ffn1_d, g_final, False)
        ya = _rnn_branch(xf, l, g_mix, w_in_b, rnn_conv_w, conv_b, wbd, gate_a_b, gate_x_b, lam, rnn_wo, seq)
        yb = _sconv_branch(xf, l, g_mix, w_in_b, sconv_w, sconv_wo, seq)
        q, k, qi, ki, vt, wit = _qkv_proj(xf, l, g_mix, w_in_b, w_t, tabs)
        oc = _attention(q, qi, wit, ki, k, vt, batch, seq)
        xf = _merge(xf, l, g_mix, w_gates, ya, yb, oc, attn_wo, w_o_b)
        xf = _ffn(xf, l, g_ffn2, ffn2_gu, ffn2_d, g_final, l == depth - 1)

    return xf.reshape(batch, seq, d)
```

```python
import functools
import math

import jax
import jax.numpy as jnp
import numpy as np
from jax import lax
from jax.experimental import pallas as pl
from jax.experimental.pallas import tpu as pltpu

F32 = jnp.float32
BF16 = jnp.bfloat16

D_MODEL = 1024
D_FF = 2816
D_RNN = 1024
RNN_BLOCKS = 16
RNN_BLOCK_W = D_RNN // RNN_BLOCKS
RNN_CONV_W = 4
RG_LRU_C = 8.0
D_SCONV = 1024
SCONV_W = 3
N_HEADS = 8
N_KV_HEADS = 2
HEAD_DIM = 128
N_IDX_HEADS = 8
IDX_DIM = 64
INDEX_TOPK_MAX = 256
ROPE_THETA = 500000.0
ROPE_FRACTION_DEN = 4
N_BRANCHES = 3
RMS_EPS = 1e-6

LANES = 128
SUBLANES = 8
VMEM_LIMIT = 56 * 1024 * 1024

FFN_TM = 1024
FFN_TF = 256
RNN_TM = 256
SCONV_TM = 512
QKV_TM = 512
MERGE_TM = 512
TAB_TM = 1024
ATT_TQ = 256
ATT_TK = 512
ATT_RB = 128

GATE_GROUP = 256
N_GATE_GROUPS = D_RNN // GATE_GROUP

NEG_BIG = -1e30
LOGIT_SCALE = (HEAD_DIM ** -0.5) * math.log2(math.e)
KEY_NEG_INF = np.int32(np.uint32(0x807FFFFF).view(np.int32))
KEY16_NEG_INF = np.int32(np.uint16(0x807F).view(np.int16))
U16_POS_ZERO = -int(KEY16_NEG_INF)


def _cparams(sem, flags=None):
    return pltpu.CompilerParams(dimension_semantics=sem, vmem_limit_bytes=VMEM_LIMIT, flags=flags)


def _layer_spec(l, shape, col=0):
    zeros = (0,) * (len(shape) - 1)
    return pl.BlockSpec((None,) + tuple(shape), lambda *_: (l,) + zeros + (col,))


def _rmsnorm_f32(x, g):
    var = jnp.mean(x * x, axis=-1, keepdims=True)
    return (x * lax.rsqrt(var + RMS_EPS)) * g


def _bdot(a, b):
    return jnp.dot(a, b, preferred_element_type=F32)


def _dot_nt(a, b):
    return lax.dot_general(a, b, (((1,), (1,)), ((), ())), preferred_element_type=F32)


def _ffn_body(x_ref, g_ref, wg_ref, wu_ref, wd_ref, fn_ref, o_ref, xn_ref, acc_ref, *, n_ff, final_norm):
    j = pl.program_id(1)

    @pl.when(j == 0)
    def _():
        xn_ref[...] = _rmsnorm_f32(x_ref[...], g_ref[...]).astype(BF16)
        acc_ref[...] = jnp.zeros_like(acc_ref)

    xn = xn_ref[...]
    g = _bdot(xn, wg_ref[...])
    u = _bdot(xn, wu_ref[...])
    a = (g * jax.nn.sigmoid(g) * u).astype(BF16)
    acc_ref[...] += _bdot(a, wd_ref[...])

    @pl.when(j == n_ff - 1)
    def _():
        y = x_ref[...] + 0.5 * acc_ref[...]
        if final_norm:
            y = _rmsnorm_f32(y, fn_ref[...])
        o_ref[...] = y


def _ffn(x, l, gain, w_gu, w_d, fnorm, final_norm):
    n = x.shape[0]
    n_ff = D_FF // FFN_TF
    grid = (n // FFN_TM, n_ff)
    return pl.pallas_call(
        functools.partial(_ffn_body, n_ff=n_ff, final_norm=final_norm),
        out_shape=jax.ShapeDtypeStruct((n, D_MODEL), F32),
        grid=grid,
        in_specs=[
            pl.BlockSpec((FFN_TM, D_MODEL), lambda i, j: (i, 0)),
            pl.BlockSpec((None, 1, D_MODEL), lambda i, j: (l, 0, 0)),
            pl.BlockSpec((None, D_MODEL, FFN_TF), lambda i, j: (l, 0, j)),
            pl.BlockSpec((None, D_MODEL, FFN_TF), lambda i, j: (l, 0, j + n_ff)),
            pl.BlockSpec((None, FFN_TF, D_MODEL), lambda i, j: (l, j, 0)),
            pl.BlockSpec((1, D_MODEL), lambda i, j: (0, 0)),
        ],
        out_specs=pl.BlockSpec((FFN_TM, D_MODEL), lambda i, j: (i, 0)),
        scratch_shapes=[pltpu.VMEM((FFN_TM, D_MODEL), BF16), pltpu.VMEM((FFN_TM, D_MODEL), F32)],
        compiler_params=_cparams(("arbitrary", "arbitrary")),
        name="ffn",
    )(x, gain, w_gu, w_gu, w_d, fnorm)


def _causal_dwconv_tile(x, prev_rows, w_ref, width):
    tm, ch = x.shape
    groups = tm // SUBLANES
    x3 = x.reshape(groups, SUBLANES, ch)
    sub = lax.broadcasted_iota(jnp.int32, x3.shape, 1)
    out = x3 * w_ref[width - 1:width, :]
    for lag in range(1, width):
        cur = pltpu.roll(x3, lag, 1)
        head = pltpu.roll(prev_rows.reshape(1, SUBLANES, ch), lag, 1)
        prev = jnp.concatenate([head, cur[:groups - 1]], axis=0)
        out = out + jnp.where(sub >= lag, cur, prev) * w_ref[width - 1 - lag:width - lag, :]
    return out.reshape(tm, ch)


def _rnn_body(x_ref, g_ref, w_ref, cw_ref, cb_ref, wbd_ref, ba_ref, bx_ref, lam_ref, wout_ref,
              o_ref, xbuf_ref, h_ref, *, tiles_per_seq):
    tm = x_ref.shape[0]
    i = pl.program_id(0)

    @pl.when(i % tiles_per_seq == 0)
    def _():
        xbuf_ref[...] = jnp.zeros_like(xbuf_ref)
        h_ref[...] = jnp.zeros_like(h_ref)

    u = _rmsnorm_f32(x_ref[...], g_ref[...]).astype(BF16)
    p = _bdot(u, w_ref[...])
    rx = p[:, :D_RNN]
    rg = p[:, D_RNN:]

    xa = cb_ref[...] + _causal_dwconv_tile(rx, xbuf_ref[...], cw_ref, RNN_CONV_W)
    xbuf_ref[...] = rx[tm - SUBLANES:, :]

    xab = xa.astype(BF16)
    r_parts, i_parts = [], []
    for jg in range(N_GATE_GROUPS):
        gj = _bdot(xab[:, jg * GATE_GROUP:(jg + 1) * GATE_GROUP], wbd_ref[jg])
        r_parts.append(gj[:, :GATE_GROUP])
        i_parts.append(gj[:, GATE_GROUP:])
    r = jax.nn.sigmoid(jnp.concatenate(r_parts, axis=1) + ba_ref[...])
    ig = jax.nn.sigmoid(jnp.concatenate(i_parts, axis=1) + bx_ref[...])

    nl = -lam_ref[...]
    sp = jnp.maximum(nl, 0.0) + jnp.log1p(jnp.exp(-jnp.abs(nl)))
    log_a = (-RG_LRU_C) * r * sp
    a = jnp.exp(log_a)
    one_m_a2 = -jnp.tanh(log_a) * (a * a + 1.0)
    in_scale = jnp.where(one_m_a2 > 0.0, one_m_a2 * lax.rsqrt(one_m_a2), 0.0)
    b = in_scale * (ig * xa)

    groups = tm // SUBLANES
    a3 = a.reshape(groups, SUBLANES, D_RNN)
    b3 = b.reshape(groups, SUBLANES, D_RNN)
    sub = lax.broadcasted_iota(jnp.int32, a3.shape, 1)
    d = 1
    while d < SUBLANES:
        keep = sub >= d
        a_s = jnp.where(keep, pltpu.roll(a3, d, 1), 1.0)
        b_s = jnp.where(keep, pltpu.roll(b3, d, 1), 0.0)
        b3 = a3 * b_s + b3
        a3 = a3 * a_s
        d *= 2
    carry = h_ref[...]
    h_groups = []
    for gi in range(groups):
        hg = b3[gi] + a3[gi] * carry
        h_groups.append(hg)
        carry = hg[SUBLANES - 1:SUBLANES, :]
    h_ref[...] = carry
    h = jnp.concatenate(h_groups, axis=0)

    gl = 0.5 * rg * (1.0 + jnp.tanh(math.sqrt(2.0 / math.pi) * (rg + 0.044715 * (rg * rg * rg))))
    o_ref[...] = _bdot((h * gl).astype(BF16), wout_ref[...])


def _rnn_branch(x, l, gain, w_in, conv_w, conv_b, wbd, ba, bx, lam, w_out, seq):
    n = x.shape[0]
    tm = RNN_TM
    return pl.pallas_call(
        functools.partial(_rnn_body, tiles_per_seq=seq // tm),
        out_shape=jax.ShapeDtypeStruct((n, D_MODEL), F32),
        grid=(n // tm,),
        in_specs=[
            pl.BlockSpec((tm, D_MODEL), lambda i: (i, 0)),
            _layer_spec(l, (1, D_MODEL)),
            _layer_spec(l, (D_MODEL, 2 * D_RNN)),
            _layer_spec(l, (RNN_CONV_W, D_RNN)),
            _layer_spec(l, (1, D_RNN)),
            _layer_spec(l, (N_GATE_GROUPS, GATE_GROUP, 2 * GATE_GROUP)),
            _layer_spec(l, (1, D_RNN)),
            _layer_spec(l, (1, D_RNN)),
            _layer_spec(l, (1, D_RNN)),
            _layer_spec(l, (D_RNN, D_MODEL)),
        ],
        out_specs=pl.BlockSpec((tm, D_MODEL), lambda i: (i, 0)),
        scratch_shapes=[pltpu.VMEM((SUBLANES, D_RNN), F32), pltpu.VMEM((1, D_RNN), F32)],
        compiler_params=_cparams(("arbitrary",)),
        name="rnn_branch",
    )(x, gain, w_in, conv_w, conv_b, wbd, ba, bx, lam, w_out)


def _sconv_body(x_ref, g_ref, wb_ref, wc_ref, wh_ref, sw_ref, wout_ref, o_ref, zbuf_ref, *, tiles_per_seq):
    tm = x_ref.shape[0]
    i = pl.program_id(0)

    @pl.when(i % tiles_per_seq == 0)
    def _():
        zbuf_ref[...] = jnp.zeros_like(zbuf_ref)

    u = _rmsnorm_f32(x_ref[...], g_ref[...]).astype(BF16)
    c_b = _bdot(u, wb_ref[...])
    z = _bdot(u, wc_ref[...]) * _bdot(u, wh_ref[...])
    cv = _causal_dwconv_tile(z, zbuf_ref[...], sw_ref, SCONV_W)
    zbuf_ref[...] = z[tm - SUBLANES:, :]
    o_ref[...] = _bdot((c_b * cv).astype(BF16), wout_ref[...])


def _sconv_branch(x, l, gain, w_in, sw, w_out, seq):
    n = x.shape[0]
    tm = SCONV_TM
    col0 = 2 * D_RNN // D_SCONV
    return pl.pallas_call(
        functools.partial(_sconv_body, tiles_per_seq=seq // tm),
        out_shape=jax.ShapeDtypeStruct((n, D_MODEL), F32),
        grid=(n // tm,),
        in_specs=[
            pl.BlockSpec((tm, D_MODEL), lambda i: (i, 0)),
            _layer_spec(l, (1, D_MODEL)),
            _layer_spec(l, (D_MODEL, D_SCONV), col0),
            _layer_spec(l, (D_MODEL, D_SCONV), col0 + 1),
            _layer_spec(l, (D_MODEL, D_SCONV), col0 + 2),
            _layer_spec(l, (SCONV_W, D_SCONV)),
            _layer_spec(l, (D_SCONV, D_MODEL)),
        ],
        out_specs=pl.BlockSpec((tm, D_MODEL), lambda i: (i, 0)),
        scratch_shapes=[pltpu.VMEM((SUBLANES, D_SCONV), F32)],
        compiler_params=_cparams(("arbitrary",)),
        name="sconv_branch",
    )(x, gain, w_in, w_in, w_in, sw, w_out)


def _tab_body(pos_ref, f128_ref, s128_ref, f64_ref, s64_ref, c128_ref, sg128_ref, c64_ref, sg64_ref):
    posf = pos_ref[...].astype(F32)
    for f_ref, s_ref, c_out, s_out in ((f128_ref, s128_ref, c128_ref, sg128_ref),
                                       (f64_ref, s64_ref, c64_ref, sg64_ref)):
        sgn = s_ref[...]
        ang = posf * f_ref[...]
        c_out[...] = jnp.where(sgn != 0.0, jnp.cos(ang), 1.0)
        s_out[...] = jnp.sin(ang) * sgn


def _rope_lane_rows(period, rot):
    half = rot // 2
    inv_freq = ROPE_THETA ** (-jnp.arange(0, rot, 2, dtype=F32) / rot)
    lane = np.arange(LANES) % period
    f_row = jnp.where(lane < rot, inv_freq[lane % half], 0.0).astype(F32)
    s_row = np.where(lane < half, -1.0, np.where(lane < rot, 1.0, 0.0)).astype(np.float32)
    return f_row.reshape(1, LANES), jnp.asarray(s_row).reshape(1, LANES)


def _rope_tables(pos_col):
    n = pos_col.shape[0]
    f128, s128 = _rope_lane_rows(HEAD_DIM, HEAD_DIM // ROPE_FRACTION_DEN)
    f64, s64 = _rope_lane_rows(IDX_DIM, IDX_DIM // ROPE_FRACTION_DEN)
    row = pl.BlockSpec((1, LANES), lambda i: (0, 0))
    tab = pl.BlockSpec((TAB_TM, LANES), lambda i: (i, 0))
    return pl.pallas_call(
        _tab_body,
        out_shape=[jax.ShapeDtypeStruct((n, LANES), F32)] * 4,
        grid=(n // TAB_TM,),
        in_specs=[pl.BlockSpec((TAB_TM, 1), lambda i: (i, 0)), row, row, row, row],
        out_specs=[tab] * 4,
        compiler_params=_cparams(("arbitrary",)),
        name="rope_tables",
    )(pos_col, f128, s128, f64, s64)


def _rope_apply(xh, c, sg, first_half, half):
    partner = jnp.where(first_half, pltpu.roll(xh, LANES - half, 1), pltpu.roll(xh, half, 1))
    return xh * c + partner * sg


def _qkv_body(x_ref, g_ref, wq_ref, wk_ref, wqi_ref, wki_ref, wt_ref, c128_ref, s128_ref, c64_ref, s64_ref,
              q_ref, k_ref, qi_ref, ki_ref, vt_ref, wit_ref):
    tm = x_ref.shape[0]
    u = _rmsnorm_f32(x_ref[...], g_ref[...]).astype(BF16)
    p = jnp.concatenate([_bdot(u, w[...]) for w in (wq_ref, wk_ref, wqi_ref, wki_ref)], axis=1)
    pt = _dot_nt(wt_ref[...], u)
    kvw = N_KV_HEADS * HEAD_DIM
    vt_ref[...] = pt[:kvw, :].astype(BF16)
    wit_ref[...] = pt[kvw:kvw + N_IDX_HEADS, :]
    lane = lax.broadcasted_iota(jnp.int32, (tm, LANES), 1)
    h128 = HEAD_DIM // ROPE_FRACTION_DEN // 2
    h64 = IDX_DIM // ROPE_FRACTION_DEN // 2
    first128 = lane < h128
    first64 = (lane % IDX_DIM) < h64
    c128, s128 = c128_ref[...], s128_ref[...]
    c64, s64 = c64_ref[...], s64_ref[...]

    off = 0
    for h in range(N_HEADS):
        blk = p[:, off + h * LANES: off + (h + 1) * LANES]
        roped = _rope_apply(blk, c128, s128, first128, h128)
        q_ref[:, h * LANES:(h + 1) * LANES] = (roped * LOGIT_SCALE).astype(BF16)
    off += N_HEADS * HEAD_DIM
    for h in range(N_KV_HEADS):
        blk = p[:, off + h * LANES: off + (h + 1) * LANES]
        k_ref[:, h * LANES:(h + 1) * LANES] = _rope_apply(blk, c128, s128, first128, h128).astype(BF16)
    off += N_KV_HEADS * HEAD_DIM
    for h in range(N_IDX_HEADS * IDX_DIM // LANES):
        blk = p[:, off + h * LANES: off + (h + 1) * LANES]
        qi_ref[:, h * LANES:(h + 1) * LANES] = _rope_apply(blk, c64, s64, first64, h64).astype(BF16)
    off += N_IDX_HEADS * IDX_DIM
    blk = p[:, off: off + LANES]
    ki_ref[...] = _rope_apply(blk, c64, s64, first64, h64)[:, :IDX_DIM].astype(BF16)


def _qkv_proj(x, l, gain, w_in, w_t, tabs):
    n = x.shape[0]
    tm = QKV_TM
    qw, kw, qiw = N_HEADS * HEAD_DIM, N_KV_HEADS * HEAD_DIM, N_IDX_HEADS * IDX_DIM
    o_q = 2 * D_RNN + 3 * D_SCONV
    o_k = o_q + qw
    o_qi = o_k + 2 * kw
    o_ki = o_qi + qiw
    tile = lambda w: pl.BlockSpec((tm, w), lambda i: (i, 0))
    ttile = lambda r: pl.BlockSpec((r, tm), lambda i: (0, i))
    kvw = N_KV_HEADS * HEAD_DIM
    return pl.pallas_call(
        _qkv_body,
        out_shape=[
            jax.ShapeDtypeStruct((n, N_HEADS * HEAD_DIM), BF16),
            jax.ShapeDtypeStruct((n, kvw), BF16),
            jax.ShapeDtypeStruct((n, N_IDX_HEADS * IDX_DIM), BF16),
            jax.ShapeDtypeStruct((n, IDX_DIM), BF16),
            jax.ShapeDtypeStruct((kvw, n), BF16),
            jax.ShapeDtypeStruct((N_IDX_HEADS, n), F32),
        ],
        grid=(n // tm,),
        in_specs=[tile(D_MODEL), _layer_spec(l, (1, D_MODEL)),
                  _layer_spec(l, (D_MODEL, qw), o_q // qw),
                  _layer_spec(l, (D_MODEL, kw), o_k // kw),
                  _layer_spec(l, (D_MODEL, qiw), o_qi // qiw),
                  _layer_spec(l, (D_MODEL, LANES), o_ki // LANES),
                  _layer_spec(l, w_t.shape[1:]),
                  tile(LANES), tile(LANES), tile(LANES), tile(LANES)],
        out_specs=[tile(N_HEADS * HEAD_DIM), tile(kvw), tile(N_IDX_HEADS * IDX_DIM), tile(IDX_DIM),
                   ttile(kvw), ttile(N_IDX_HEADS)],
        compiler_params=_cparams(("arbitrary",)),
        name="qkv_proj",
    )(x, gain, w_in, w_in, w_in, w_in, w_t, *tabs)


def _key_to_float(u):
    key = u + KEY_NEG_INF
    bits = jnp.where(key < 0, key ^ jnp.int32(0x7FFFFFFF), key)
    return lax.bitcast_convert_type(bits, F32)


def _float_to_key(x):
    bits = lax.bitcast_convert_type(x, jnp.int32)
    key = jnp.where(bits < 0, bits ^ jnp.int32(0x7FFFFFFF), bits)
    return key - KEY_NEG_INF


def _key16_to_float(u16):
    key = ((u16 + (KEY16_NEG_INF + 0x8000)) & 0xFFFF) - 0x8000
    bits = jnp.where(key < 0, key ^ jnp.int32(0x7FFF), key)
    return lax.bitcast_convert_type(lax.shift_left(bits, 16), F32)


def _attn_body(q_ref, qi_ref, wit_ref, ki_ref, k_ref, vt_ref, tri_ref, o_ref,
               sc_ref, sh_ref, m_ref, l_ref, acc_ref, need_ref, run_ref, bias_ref, lg_ref, p_ref, *, seq, top_k):
    tq, tk = ATT_TQ, ATT_TK
    j = pl.program_id(1)
    q0 = j * tq
    nch = (q0 + tq - 1) // tk + 1
    tpos = q0 + lax.broadcasted_iota(jnp.int32, (1, tq), 1)
    w_scale = (N_IDX_HEADS ** -0.5) * (IDX_DIM ** -0.5)

    def chunk_rows(c):
        return pl.ds(pl.multiple_of(c * tk, tk), tk)

    def key_pos(c):
        return c * tk + lax.broadcasted_iota(jnp.int32, (tk, tq), 0)

    def fold_rows(x):
        groups = tk // SUBLANES
        side = int(math.isqrt(groups))
        x4 = x.reshape(groups // side, side, SUBLANES, tq)
        return jnp.sum(jnp.sum(x4, axis=1), axis=0)

    wsc = wit_ref[...] * w_scale

    def score_chunk(c, carry):
        for rb in range(tk // ATT_RB):
            r0 = pl.multiple_of(c * tk + rb * ATT_RB, ATT_RB)
            kic = ki_ref[pl.ds(r0, ATT_RB), :]
            acc = jnp.zeros((ATT_RB, tq), F32)
            for h in range(N_IDX_HEADS):
                rel = _dot_nt(kic, qi_ref[:, h * IDX_DIM:(h + 1) * IDX_DIM])
                acc = acc + wsc[h:h + 1, :] * jnp.maximum(rel, 0.0)
            kp = r0 + lax.broadcasted_iota(jnp.int32, (ATT_RB, tq), 0)
            masked = jnp.where(kp <= tpos, acc, -jnp.inf)
            sc_ref[pl.ds(r0, ATT_RB), :] = masked
            sh_ref[pl.ds(r0, ATT_RB), :] = masked.astype(BF16)
        return carry

    lax.fori_loop(0, nch, score_chunk, 0)

    def count(pred):
        def body(c, acc):
            return acc + fold_rows(jnp.where(pred(sc_ref[chunk_rows(c), :], c), 1.0, 0.0))
        acc = lax.fori_loop(0, nch, body, jnp.zeros((SUBLANES, tq), F32))
        return jnp.sum(acc, axis=0, keepdims=True)

    def count_rounded(gb):
        pack = 2 * SUBLANES
        one, zero = jnp.ones((), BF16), jnp.zeros((), BF16)

        def body(c, acc):
            ind = jnp.where(sh_ref[chunk_rows(c), :] >= gb, one, zero)
            parts = [ind[r * pack:(r + 1) * pack, :] for r in range(tk // pack)]
            while len(parts) > 1:
                parts = [a + b for a, b in zip(parts[0::2], parts[1::2])]
            return acc + parts[0]
        acc = lax.fori_loop(0, nch, body, jnp.zeros((pack, tq), BF16))
        return jnp.sum(acc.astype(F32), axis=0, keepdims=True)

    kf = float(top_k)

    def coarse_step(it, u16):
        cand = u16 | lax.shift_left(jnp.int32(1), 15 - it)
        cnt = count_rounded(_key16_to_float(cand).astype(BF16))
        return jnp.where(cnt >= kf, cand, u16)

    u16 = lax.fori_loop(0, 16, coarse_step, jnp.zeros((1, tq), jnp.int32))
    below = jnp.where(u16 == U16_POS_ZERO, U16_POS_ZERO - 2, jnp.maximum(u16 - 1, 0))
    base = _float_to_key(_key16_to_float(below))

    def fine_step(it, carry):
        delta, cnt_u = carry
        cand = delta | lax.shift_left(jnp.int32(1), 16 - it)
        cf = _key_to_float(base + cand)
        cnt = count(lambda s, c: s >= cf)
        ok = cnt >= kf
        return jnp.where(ok, cand, delta), jnp.where(ok, cnt, cnt_u)

    base_f = _key_to_float(base)
    cnt_base = count(lambda s, c: s >= base_f)
    delta, cnt_ge = lax.fori_loop(0, 17, fine_step, (jnp.zeros((1, tq), jnp.int32), cnt_base))
    thr = _key_to_float(base + delta)

    tie = (cnt_ge > kf) & (thr > -jnp.inf)
    any_tie = jnp.sum(jnp.where(tie, 1.0, 0.0)) > 0.0

    @pl.when(any_tie)
    def _():
        cnt_gt = count(lambda s, c: s > thr)
        need_ref[...] = jnp.broadcast_to(kf - cnt_gt, need_ref.shape)
        run_ref[...] = jnp.zeros_like(run_ref)

    m_ref[...] = jnp.full(m_ref.shape, NEG_BIG, F32)
    l_ref[...] = jnp.zeros_like(l_ref)
    acc_ref[...] = jnp.zeros_like(acc_ref)
    hpg = N_HEADS // N_KV_HEADS

    rb_rows = ATT_RB
    nrb = tk // rb_rows
    pv_rows = 2 * rb_rows

    def attn_chunk(c, carry):
        def rows(rb, n=rb_rows):
            return pl.ds(pl.multiple_of(c * tk + rb * rb_rows, rb_rows), n)

        def blk(rb, n=rb_rows):
            return slice(rb * rb_rows, rb * rb_rows + n)

        def fold(x, op):
            return op(x.reshape(rb_rows // SUBLANES, SUBLANES, tq), axis=0)

        def qk_block(h, rb, bias):
            g = h // hpg
            kc = k_ref[rows(rb), g * HEAD_DIM:(g + 1) * HEAD_DIM]
            return _dot_nt(kc, q_ref[:, h * HEAD_DIM:(h + 1) * HEAD_DIM]) + bias

        def pv_half(h, slot, half, alpha):
            g = h // hpg
            cols = pl.ds(pl.multiple_of(c * tk + half * pv_rows, pv_rows), pv_rows)
            vtc = vt_ref[g * HEAD_DIM:(g + 1) * HEAD_DIM, cols]
            t = _bdot(vtc, p_ref[slot, blk(half * 2, pv_rows), :])
            if half == 0:
                acc_ref[h] = alpha * acc_ref[h] + t
            else:
                acc_ref[h] = acc_ref[h] + t

        def causal(rb):
            kp = c * tk + rb * rb_rows + lax.broadcasted_iota(jnp.int32, (rb_rows, tq), 0)
            return kp <= tpos

        @pl.when(jnp.logical_not(any_tie))
        def _():
            for rb in range(nrb):
                sel = causal(rb) & (sc_ref[rows(rb), :] >= thr)
                bias_ref[blk(rb), :] = jnp.where(sel, 0.0, NEG_BIG)

        @pl.when(any_tie)
        def _():
            need = need_ref[0:1, :]
            run = run_ref[0:1, :]
            for rb in range(nrb):
                s = sc_ref[rows(rb), :]
                eq = s == thr
                eqf = jnp.where(eq, 1.0, 0.0)
                incl = _bdot(tri_ref[...], eqf.astype(BF16))
                before = run + (incl - eqf)
                sel = causal(rb) & ((s > thr) | (eq & (before < need)))
                bias_ref[blk(rb), :] = jnp.where(sel, 0.0, NEG_BIG)
                run = run + incl[rb_rows - 1:rb_rows, :]
            run_ref[...] = jnp.broadcast_to(run, run_ref.shape)

        cm = None
        for rb in range(nrb):
            lg = qk_block(0, rb, bias_ref[blk(rb), :])
            lg_ref[0, blk(rb), :] = lg
            part = fold(lg, jnp.max)
            cm = part if cm is None else jnp.maximum(cm, part)

        alpha_prev = None
        for h in range(N_HEADS):
            cur = h % 2
            m_prev = m_ref[h]
            m_new = jnp.maximum(m_prev, jnp.max(cm, axis=0, keepdims=True))
            alpha = jnp.exp2(m_prev - m_new)
            m_ref[h] = m_new
            cm = None
            lsum = None
            for rb in range(nrb):
                if h + 1 < N_HEADS:
                    lg = qk_block(h + 1, rb, bias_ref[blk(rb), :])
                    lg_ref[1 - cur, blk(rb), :] = lg
                    part = fold(lg, jnp.max)
                    cm = part if cm is None else jnp.maximum(cm, part)
                pr = jnp.exp2(lg_ref[cur, blk(rb), :] - m_new)
                part = fold(pr, jnp.sum)
                lsum = part if lsum is None else lsum + part
                p_ref[cur, blk(rb), :] = pr.astype(BF16)
                if h > 0 and rb % 2 == 0 and rb // 2 < tk // pv_rows:
                    pv_half(h - 1, 1 - cur, rb // 2, alpha_prev)
            l_ref[h] = alpha * l_ref[h] + lsum
            alpha_prev = alpha
        last = N_HEADS - 1
        for half in range(tk // pv_rows):
            pv_half(last, last % 2, half, alpha_prev)
        return carry

    lax.fori_loop(0, nch, attn_chunk, 0)

    for h in range(N_HEADS):
        out_t = acc_ref[h] / jnp.sum(l_ref[h], axis=0, keepdims=True)
        o_ref[:, h * HEAD_DIM:(h + 1) * HEAD_DIM] = out_t.T.astype(BF16)


def _attention(q, qi, wit, ki, k, vt, batch, seq):
    n = q.shape[0]
    tq, tk = ATT_TQ, ATT_TK
    nq = seq // tq
    top_k = min(INDEX_TOPK_MAX, seq // 4)
    kvw = N_KV_HEADS * HEAD_DIM
    assert seq % tk == 0 and tk % tq == 0
    assert seq // (2 * SUBLANES) <= 256
    qtile = lambda w: pl.BlockSpec((tq, w), lambda b, j: (b * nq + j, 0))
    whole = lambda w: pl.BlockSpec((seq, w), lambda b, j: (b, 0))
    return pl.pallas_call(
        functools.partial(_attn_body, seq=seq, top_k=top_k),
        out_shape=jax.ShapeDtypeStruct((n, N_HEADS * HEAD_DIM), BF16),
        grid=(batch, nq),
        in_specs=[qtile(N_HEADS * HEAD_DIM), qtile(N_IDX_HEADS * IDX_DIM),
                  pl.BlockSpec((N_IDX_HEADS, tq), lambda b, j: (0, b * nq + j)),
                  whole(IDX_DIM), whole(kvw),
                  pl.BlockSpec((kvw, seq), lambda b, j: (0, b)),
                  pl.BlockSpec((ATT_RB, ATT_RB), lambda b, j: (0, 0))],
        out_specs=qtile(N_HEADS * HEAD_DIM),
        scratch_shapes=[
            pltpu.VMEM((seq, tq), F32),
            pltpu.VMEM((seq, tq), BF16),
            pltpu.VMEM((N_HEADS, 1, tq), F32),
            pltpu.VMEM((N_HEADS, SUBLANES, tq), F32),
            pltpu.VMEM((N_HEADS, HEAD_DIM, tq), F32),
            pltpu.VMEM((SUBLANES, tq), F32),
            pltpu.VMEM((SUBLANES, tq), F32),
            pltpu.VMEM((tk, tq), F32),
            pltpu.VMEM((2, tk, tq), F32),
            pltpu.VMEM((2, tk, tq), BF16),
        ],
        compiler_params=_cparams(("arbitrary", "arbitrary")),
        name="dsa_attention",
    )(q, qi, wit, ki, k, vt, jnp.tril(jnp.ones((ATT_RB, ATT_RB), BF16)))


def _merge_body(x_ref, g_ref, wg_ref, ya_ref, yb_ref, oc_ref, wao_ref, wo_ref, o_ref):
    x = x_ref[...]
    u = _rmsnorm_f32(x, g_ref[...]).astype(BF16)
    gt = jax.nn.sigmoid(_bdot(u, wg_ref[...]))
    yc = _bdot(oc_ref[...], wao_ref[...])
    merged = (gt[:, :D_MODEL] * ya_ref[...] + gt[:, D_MODEL:2 * D_MODEL] * yb_ref[...]
              + gt[:, 2 * D_MODEL:] * yc)
    o_ref[...] = x + _bdot(merged.astype(BF16), wo_ref[...])


def _merge(x, l, gain, w_gates, ya, yb, oc, w_attn_out, w_o):
    n = x.shape[0]
    tm = MERGE_TM
    tile = pl.BlockSpec((tm, D_MODEL), lambda i: (i, 0))
    return pl.pallas_call(
        _merge_body,
        out_shape=jax.ShapeDtypeStruct((n, D_MODEL), F32),
        grid=(n // tm,),
        in_specs=[tile, _layer_spec(l, (1, D_MODEL)), _layer_spec(l, (D_MODEL, N_BRANCHES * D_MODEL)),
                  tile, tile, tile,
                  _layer_spec(l, (N_HEADS * HEAD_DIM, D_MODEL)), _layer_spec(l, (D_MODEL, D_MODEL))],
        out_specs=tile,
        compiler_params=_cparams(("arbitrary",)),
        name="merge",
    )(x, gain, w_gates, ya, yb, oc, w_attn_out, w_o)


def _block_diag_groups(w):
    depth = w.shape[0]
    per = GATE_GROUP // RNN_BLOCK_W
    w5 = w.reshape(depth, N_GATE_GROUPS, per, RNN_BLOCK_W, RNN_BLOCK_W)
    eye = jnp.eye(per, dtype=w.dtype)
    return jnp.einsum("lgnij,nm->lgnimj", w5, eye).reshape(depth, N_GATE_GROUPS, GATE_GROUP, GATE_GROUP)


def kernel(x, positions, ffn1_norm, ffn1_w_gate_up, ffn1_w_down, mix_norm, w_in, rnn_conv_w, rnn_conv_b,
           rnn_gate_a_w, rnn_gate_a_b, rnn_gate_x_w, rnn_gate_x_b, rnn_lambda, rnn_w_out, sconv_w,
           sconv_w_out, attn_w_out, w_o, ffn2_norm, ffn2_w_gate_up, ffn2_w_down, final_norm):
    batch, seq, d = x.shape
    depth = w_in.shape[0]
    n = batch * seq
    xf = x.reshape(n, d)
    rows = lambda v: v.reshape(depth, 1, -1)
    bf = lambda w: w.astype(BF16)

    tabs = _rope_tables(positions.reshape(n, 1))

    o_v = 2 * D_RNN + 3 * D_SCONV + (N_HEADS + N_KV_HEADS) * HEAD_DIM
    o_qi = o_v + N_KV_HEADS * HEAD_DIM
    o_wi = o_qi + N_IDX_HEADS * IDX_DIM + IDX_DIM
    o_gates = o_wi + N_IDX_HEADS

    w_in_b = bf(w_in)
    w_gates = bf(w_in[:, :, o_gates:])
    w_t = bf(jnp.swapaxes(jnp.concatenate([w_in[:, :, o_v:o_qi], w_in[:, :, o_wi:o_gates]], axis=2), 1, 2))
    wbd = bf(jnp.concatenate([_block_diag_groups(rnn_gate_a_w), _block_diag_groups(rnn_gate_x_w)], axis=-1))
    ffn1_gu, ffn1_d, ffn2_gu, ffn2_d = bf(ffn1_w_gate_up), bf(ffn1_w_down), bf(ffn2_w_gate_up), bf(ffn2_w_down)
    rnn_wo, sconv_wo, attn_wo, w_o_b = bf(rnn_w_out), bf(sconv_w_out), bf(attn_w_out), bf(w_o)
    g_ffn1, g_mix, g_ffn2 = rows(ffn1_norm), rows(mix_norm), rows(ffn2_norm)
    conv_b, gate_a_b, gate_x_b, lam = rows(rnn_conv_b), rows(rnn_gate_a_b), rows(rnn_gate_x_b), rows(rnn_lambda)
    g_final = final_norm.reshape(1, -1)

    for l in range(depth):
        xf = _ffn(xf, l, g_ffn1, ffn1_gu, ffn1_d, g_final, False)
        ya = _rnn_branch(xf, l, g_mix, w_in_b, rnn_conv_w, conv_b, wbd, gate_a_b, gate_x_b, lam, rnn_wo, seq)
        yb = _sconv_branch(xf, l, g_mix, w_in_b, sconv_w, sconv_wo, seq)
        q, k, qi, ki, vt, wit = _qkv_proj(xf, l, g_mix, w_in_b, w_t, tabs)
        oc = _attention(q, qi, wit, ki, k, vt, batch, seq)
        xf = _merge(xf, l, g_mix, w_gates, ya, yb, oc, attn_wo, w_o_b)
        xf = _ffn(xf, l, g_ffn2, ffn2_gu, ffn2_d, g_final, l == depth - 1)

    return xf.reshape(batch, seq, d)
```

```python
import functools
import math

import jax
import jax.numpy as jnp
import numpy as np
from jax import lax
from jax.experimental import pallas as pl
from jax.experimental.pallas import tpu as pltpu

F32 = jnp.float32
BF16 = jnp.bfloat16

D_MODEL = 1024
D_FF = 2816
D_RNN = 1024
RNN_BLOCKS = 16
RNN_BLOCK_W = D_RNN // RNN_BLOCKS
RNN_CONV_W = 4
RG_LRU_C = 8.0
D_SCONV = 1024
SCONV_W = 3
N_HEADS = 8
N_KV_HEADS = 2
HEAD_DIM = 128
N_IDX_HEADS = 8
IDX_DIM = 64
INDEX_TOPK_MAX = 256
ROPE_THETA = 500000.0
ROPE_FRACTION_DEN = 4
N_BRANCHES = 3
RMS_EPS = 1e-6

LANES = 128
SUBLANES = 8
VMEM_LIMIT = 56 * 1024 * 1024

FFN_TM = 1024
FFN_TF = 256
RNN_TM = 256
SCONV_TM = 512
QKV_TM = 512
MERGE_TM = 512
TAB_TM = 1024
ATT_TQ = 256
ATT_TK = 512
ATT_RB = 128

GATE_GROUP = 256
N_GATE_GROUPS = D_RNN // GATE_GROUP

NEG_BIG = -1e30
LOGIT_SCALE = (HEAD_DIM ** -0.5) * math.log2(math.e)
KEY_NEG_INF = np.int32(np.uint32(0x807FFFFF).view(np.int32))
KEY_SPAN = 0x7F800000 - int(KEY_NEG_INF)
KEY16_NEG_INF = np.int32(np.uint16(0x807F).view(np.int16))
KEY16_SPAN = 0x7F80 - int(KEY16_NEG_INF)
U16_POS_ZERO = -int(KEY16_NEG_INF)


def _cparams(sem, flags=None):
    return pltpu.CompilerParams(dimension_semantics=sem, vmem_limit_bytes=VMEM_LIMIT, flags=flags)


def _layer_spec(l, shape, col=0):
    zeros = (0,) * (len(shape) - 1)
    return pl.BlockSpec((None,) + tuple(shape), lambda *_: (l,) + zeros + (col,))


def _rmsnorm_f32(x, g):
    var = jnp.mean(x * x, axis=-1, keepdims=True)
    return (x * lax.rsqrt(var + RMS_EPS)) * g


def _bdot(a, b):
    return jnp.dot(a, b, preferred_element_type=F32)


def _dot_nt(a, b):
    return lax.dot_general(a, b, (((1,), (1,)), ((), ())), preferred_element_type=F32)


def _ffn_body(x_ref, g_ref, wg_ref, wu_ref, wd_ref, fn_ref, o_ref, xn_ref, acc_ref, *, n_ff, final_norm):
    j = pl.program_id(1)

    @pl.when(j == 0)
    def _():
        xn_ref[...] = _rmsnorm_f32(x_ref[...], g_ref[...]).astype(BF16)
        acc_ref[...] = jnp.zeros_like(acc_ref)

    xn = xn_ref[...]
    g = _bdot(xn, wg_ref[...])
    u = _bdot(xn, wu_ref[...])
    a = (g * jax.nn.sigmoid(g) * u).astype(BF16)
    acc_ref[...] += _bdot(a, wd_ref[...])

    @pl.when(j == n_ff - 1)
    def _():
        y = x_ref[...] + 0.5 * acc_ref[...]
        if final_norm:
            y = _rmsnorm_f32(y, fn_ref[...])
        o_ref[...] = y


def _ffn(x, l, gain, w_gu, w_d, fnorm, final_norm):
    n = x.shape[0]
    n_ff = D_FF // FFN_TF
    grid = (n // FFN_TM, n_ff)
    return pl.pallas_call(
        functools.partial(_ffn_body, n_ff=n_ff, final_norm=final_norm),
        out_shape=jax.ShapeDtypeStruct((n, D_MODEL), F32),
        grid=grid,
        in_specs=[
            pl.BlockSpec((FFN_TM, D_MODEL), lambda i, j: (i, 0)),
            pl.BlockSpec((None, 1, D_MODEL), lambda i, j: (l, 0, 0)),
            pl.BlockSpec((None, D_MODEL, FFN_TF), lambda i, j: (l, 0, j)),
            pl.BlockSpec((None, D_MODEL, FFN_TF), lambda i, j: (l, 0, j + n_ff)),
            pl.BlockSpec((None, FFN_TF, D_MODEL), lambda i, j: (l, j, 0)),
            pl.BlockSpec((1, D_MODEL), lambda i, j: (0, 0)),
        ],
        out_specs=pl.BlockSpec((FFN_TM, D_MODEL), lambda i, j: (i, 0)),
        scratch_shapes=[pltpu.VMEM((FFN_TM, D_MODEL), BF16), pltpu.VMEM((FFN_TM, D_MODEL), F32)],
        compiler_params=_cparams(("arbitrary", "arbitrary")),
        name="ffn",
    )(x, gain, w_gu, w_gu, w_d, fnorm)


def _causal_dwconv_tile(x, prev_rows, w_ref, width):
    tm, ch = x.shape
    groups = tm // SUBLANES
    x3 = x.reshape(groups, SUBLANES, ch)
    sub = lax.broadcasted_iota(jnp.int32, x3.shape, 1)
    out = x3 * w_ref[width - 1:width, :]
    for lag in range(1, width):
        cur = pltpu.roll(x3, lag, 1)
        head = pltpu.roll(prev_rows.reshape(1, SUBLANES, ch), lag, 1)
        prev = jnp.concatenate([head, cur[:groups - 1]], axis=0)
        out = out + jnp.where(sub >= lag, cur, prev) * w_ref[width - 1 - lag:width - lag, :]
    return out.reshape(tm, ch)


def _rnn_body(x_ref, g_ref, w_ref, cw_ref, cb_ref, wbd_ref, ba_ref, bx_ref, lam_ref, wout_ref,
              o_ref, xbuf_ref, h_ref, *, tiles_per_seq):
    tm = x_ref.shape[0]
    i = pl.program_id(0)

    @pl.when(i % tiles_per_seq == 0)
    def _():
        xbuf_ref[...] = jnp.zeros_like(xbuf_ref)
        h_ref[...] = jnp.zeros_like(h_ref)

    u = _rmsnorm_f32(x_ref[...], g_ref[...]).astype(BF16)
    p = _bdot(u, w_ref[...])
    rx = p[:, :D_RNN]
    rg = p[:, D_RNN:]

    xa = cb_ref[...] + _causal_dwconv_tile(rx, xbuf_ref[...], cw_ref, RNN_CONV_W)
    xbuf_ref[...] = rx[tm - SUBLANES:, :]

    xab = xa.astype(BF16)
    r_parts, i_parts = [], []
    for jg in range(N_GATE_GROUPS):
        gj = _bdot(xab[:, jg * GATE_GROUP:(jg + 1) * GATE_GROUP], wbd_ref[jg])
        r_parts.append(gj[:, :GATE_GROUP])
        i_parts.append(gj[:, GATE_GROUP:])
    r = jax.nn.sigmoid(jnp.concatenate(r_parts, axis=1) + ba_ref[...])
    ig = jax.nn.sigmoid(jnp.concatenate(i_parts, axis=1) + bx_ref[...])

    nl = -lam_ref[...]
    sp = jnp.maximum(nl, 0.0) + jnp.log1p(jnp.exp(-jnp.abs(nl)))
    log_a = (-RG_LRU_C) * r * sp
    a = jnp.exp(log_a)
    one_m_a2 = -jnp.tanh(log_a) * (a * a + 1.0)
    in_scale = jnp.where(one_m_a2 > 0.0, one_m_a2 * lax.rsqrt(one_m_a2), 0.0)
    b = in_scale * (ig * xa)

    groups = tm // SUBLANES
    a3 = a.reshape(groups, SUBLANES, D_RNN)
    b3 = b.reshape(groups, SUBLANES, D_RNN)
    sub = lax.broadcasted_iota(jnp.int32, a3.shape, 1)
    d = 1
    while d < SUBLANES:
        keep = sub >= d
        a_s = jnp.where(keep, pltpu.roll(a3, d, 1), 1.0)
        b_s = jnp.where(keep, pltpu.roll(b3, d, 1), 0.0)
        b3 = a3 * b_s + b3
        a3 = a3 * a_s
        d *= 2
    carry = h_ref[...]
    h_groups = []
    for gi in range(groups):
        hg = b3[gi] + a3[gi] * carry
        h_groups.append(hg)
        carry = hg[SUBLANES - 1:SUBLANES, :]
    h_ref[...] = carry
    h = jnp.concatenate(h_groups, axis=0)

    gl = 0.5 * rg * (1.0 + jnp.tanh(math.sqrt(2.0 / math.pi) * (rg + 0.044715 * (rg * rg * rg))))
    o_ref[...] = _bdot((h * gl).astype(BF16), wout_ref[...])


def _rnn_branch(x, l, gain, w_in, conv_w, conv_b, wbd, ba, bx, lam, w_out, seq):
    n = x.shape[0]
    tm = RNN_TM
    return pl.pallas_call(
        functools.partial(_rnn_body, tiles_per_seq=seq // tm),
        out_shape=jax.ShapeDtypeStruct((n, D_MODEL), F32),
        grid=(n // tm,),
        in_specs=[
            pl.BlockSpec((tm, D_MODEL), lambda i: (i, 0)),
            _layer_spec(l, (1, D_MODEL)),
            _layer_spec(l, (D_MODEL, 2 * D_RNN)),
            _layer_spec(l, (RNN_CONV_W, D_RNN)),
            _layer_spec(l, (1, D_RNN)),
            _layer_spec(l, (N_GATE_GROUPS, GATE_GROUP, 2 * GATE_GROUP)),
            _layer_spec(l, (1, D_RNN)),
            _layer_spec(l, (1, D_RNN)),
            _layer_spec(l, (1, D_RNN)),
            _layer_spec(l, (D_RNN, D_MODEL)),
        ],
        out_specs=pl.BlockSpec((tm, D_MODEL), lambda i: (i, 0)),
        scratch_shapes=[pltpu.VMEM((SUBLANES, D_RNN), F32), pltpu.VMEM((1, D_RNN), F32)],
        compiler_params=_cparams(("arbitrary",)),
        name="rnn_branch",
    )(x, gain, w_in, conv_w, conv_b, wbd, ba, bx, lam, w_out)


def _sconv_body(x_ref, g_ref, wb_ref, wc_ref, wh_ref, sw_ref, wout_ref, o_ref, zbuf_ref, *, tiles_per_seq):
    tm = x_ref.shape[0]
    i = pl.program_id(0)

    @pl.when(i % tiles_per_seq == 0)
    def _():
        zbuf_ref[...] = jnp.zeros_like(zbuf_ref)

    u = _rmsnorm_f32(x_ref[...], g_ref[...]).astype(BF16)
    c_b = _bdot(u, wb_ref[...])
    z = _bdot(u, wc_ref[...]) * _bdot(u, wh_ref[...])
    cv = _causal_dwconv_tile(z, zbuf_ref[...], sw_ref, SCONV_W)
    zbuf_ref[...] = z[tm - SUBLANES:, :]
    o_ref[...] = _bdot((c_b * cv).astype(BF16), wout_ref[...])


def _sconv_branch(x, l, gain, w_in, sw, w_out, seq):
    n = x.shape[0]
    tm = SCONV_TM
    col0 = 2 * D_RNN // D_SCONV
    return pl.pallas_call(
        functools.partial(_sconv_body, tiles_per_seq=seq // tm),
        out_shape=jax.ShapeDtypeStruct((n, D_MODEL), F32),
        grid=(n // tm,),
        in_specs=[
            pl.BlockSpec((tm, D_MODEL), lambda i: (i, 0)),
            _layer_spec(l, (1, D_MODEL)),
            _layer_spec(l, (D_MODEL, D_SCONV), col0),
            _layer_spec(l, (D_MODEL, D_SCONV), col0 + 1),
            _layer_spec(l, (D_MODEL, D_SCONV), col0 + 2),
            _layer_spec(l, (SCONV_W, D_SCONV)),
            _layer_spec(l, (D_SCONV, D_MODEL)),
        ],
        out_specs=pl.BlockSpec((tm, D_MODEL), lambda i: (i, 0)),
        scratch_shapes=[pltpu.VMEM((SUBLANES, D_SCONV), F32)],
        compiler_params=_cparams(("arbitrary",)),
        name="sconv_branch",
    )(x, gain, w_in, w_in, w_in, sw, w_out)


def _tab_body(pos_ref, f128_ref, s128_ref, f64_ref, s64_ref, c128_ref, sg128_ref, c64_ref, sg64_ref):
    posf = pos_ref[...].astype(F32)
    for f_ref, s_ref, c_out, s_out in ((f128_ref, s128_ref, c128_ref, sg128_ref),
                                       (f64_ref, s64_ref, c64_ref, sg64_ref)):
        sgn = s_ref[...]
        ang = posf * f_ref[...]
        c_out[...] = jnp.where(sgn != 0.0, jnp.cos(ang), 1.0)
        s_out[...] = jnp.sin(ang) * sgn


def _rope_lane_rows(period, rot):
    half = rot // 2
    inv_freq = ROPE_THETA ** (-jnp.arange(0, rot, 2, dtype=F32) / rot)
    lane = np.arange(LANES) % period
    f_row = jnp.where(lane < rot, inv_freq[lane % half], 0.0).astype(F32)
    s_row = np.where(lane < half, -1.0, np.where(lane < rot, 1.0, 0.0)).astype(np.float32)
    return f_row.reshape(1, LANES), jnp.asarray(s_row).reshape(1, LANES)


def _rope_tables(pos_col):
    n = pos_col.shape[0]
    f128, s128 = _rope_lane_rows(HEAD_DIM, HEAD_DIM // ROPE_FRACTION_DEN)
    f64, s64 = _rope_lane_rows(IDX_DIM, IDX_DIM // ROPE_FRACTION_DEN)
    row = pl.BlockSpec((1, LANES), lambda i: (0, 0))
    tab = pl.BlockSpec((TAB_TM, LANES), lambda i: (i, 0))
    return pl.pallas_call(
        _tab_body,
        out_shape=[jax.ShapeDtypeStruct((n, LANES), F32)] * 4,
        grid=(n // TAB_TM,),
        in_specs=[pl.BlockSpec((TAB_TM, 1), lambda i: (i, 0)), row, row, row, row],
        out_specs=[tab] * 4,
        compiler_params=_cparams(("arbitrary",)),
        name="rope_tables",
    )(pos_col, f128, s128, f64, s64)


def _rope_apply(xh, c, sg, first_half, half):
    partner = jnp.where(first_half, pltpu.roll(xh, LANES - half, 1), pltpu.roll(xh, half, 1))
    return xh * c + partner * sg


def _qkv_body(x_ref, g_ref, wq_ref, wk_ref, wqi_ref, wki_ref, wt_ref, c128_ref, s128_ref, c64_ref, s64_ref,
              q_ref, k_ref, qi_ref, ki_ref, vt_ref, wit_ref):
    tm = x_ref.shape[0]
    u = _rmsnorm_f32(x_ref[...], g_ref[...]).astype(BF16)
    p = jnp.concatenate([_bdot(u, w[...]) for w in (wq_ref, wk_ref, wqi_ref, wki_ref)], axis=1)
    pt = _dot_nt(wt_ref[...], u)
    kvw = N_KV_HEADS * HEAD_DIM
    vt_ref[...] = pt[:kvw, :].astype(BF16)
    wit_ref[...] = pt[kvw:kvw + N_IDX_HEADS, :]
    lane = lax.broadcasted_iota(jnp.int32, (tm, LANES), 1)
    h128 = HEAD_DIM // ROPE_FRACTION_DEN // 2
    h64 = IDX_DIM // ROPE_FRACTION_DEN // 2
    first128 = lane < h128
    first64 = (lane % IDX_DIM) < h64
    c128, s128 = c128_ref[...], s128_ref[...]
    c64, s64 = c64_ref[...], s64_ref[...]

    off = 0
    for h in range(N_HEADS):
        blk = p[:, off + h * LANES: off + (h + 1) * LANES]
        roped = _rope_apply(blk, c128, s128, first128, h128)
        q_ref[:, h * LANES:(h + 1) * LANES] = (roped * LOGIT_SCALE).astype(BF16)
    off += N_HEADS * HEAD_DIM
    for h in range(N_KV_HEADS):
        blk = p[:, off + h * LANES: off + (h + 1) * LANES]
        k_ref[:, h * LANES:(h + 1) * LANES] = _rope_apply(blk, c128, s128, first128, h128).astype(BF16)
    off += N_KV_HEADS * HEAD_DIM
    for h in range(N_IDX_HEADS * IDX_DIM // LANES):
        blk = p[:, off + h * LANES: off + (h + 1) * LANES]
        qi_ref[:, h * LANES:(h + 1) * LANES] = _rope_apply(blk, c64, s64, first64, h64).astype(BF16)
    off += N_IDX_HEADS * IDX_DIM
    blk = p[:, off: off + LANES]
    ki_ref[...] = _rope_apply(blk, c64, s64, first64, h64)[:, :IDX_DIM].astype(BF16)


def _qkv_proj(x, l, gain, w_in, w_t, tabs):
    n = x.shape[0]
    tm = QKV_TM
    qw, kw, qiw = N_HEADS * HEAD_DIM, N_KV_HEADS * HEAD_DIM, N_IDX_HEADS * IDX_DIM
    o_q = 2 * D_RNN + 3 * D_SCONV
    o_k = o_q + qw
    o_qi = o_k + 2 * kw
    o_ki = o_qi + qiw
    tile = lambda w: pl.BlockSpec((tm, w), lambda i: (i, 0))
    ttile = lambda r: pl.BlockSpec((r, tm), lambda i: (0, i))
    kvw = N_KV_HEADS * HEAD_DIM
    return pl.pallas_call(
        _qkv_body,
        out_shape=[
            jax.ShapeDtypeStruct((n, N_HEADS * HEAD_DIM), BF16),
            jax.ShapeDtypeStruct((n, kvw), BF16),
            jax.ShapeDtypeStruct((n, N_IDX_HEADS * IDX_DIM), BF16),
            jax.ShapeDtypeStruct((n, IDX_DIM), BF16),
            jax.ShapeDtypeStruct((kvw, n), BF16),
            jax.ShapeDtypeStruct((N_IDX_HEADS, n), F32),
        ],
        grid=(n // tm,),
        in_specs=[tile(D_MODEL), _layer_spec(l, (1, D_MODEL)),
                  _layer_spec(l, (D_MODEL, qw), o_q // qw),
                  _layer_spec(l, (D_MODEL, kw), o_k // kw),
                  _layer_spec(l, (D_MODEL, qiw), o_qi // qiw),
                  _layer_spec(l, (D_MODEL, LANES), o_ki // LANES),
                  _layer_spec(l, w_t.shape[1:]),
                  tile(LANES), tile(LANES), tile(LANES), tile(LANES)],
        out_specs=[tile(N_HEADS * HEAD_DIM), tile(kvw), tile(N_IDX_HEADS * IDX_DIM), tile(IDX_DIM),
                   ttile(kvw), ttile(N_IDX_HEADS)],
        compiler_params=_cparams(("arbitrary",)),
        name="qkv_proj",
    )(x, gain, w_in, w_in, w_in, w_in, w_t, *tabs)


def _key_valid(u):
    return (u ^ jnp.int32(-2 ** 31)) <= jnp.int32(KEY_SPAN - 2 ** 31)


def _key_to_float(u):
    key = u + KEY_NEG_INF
    bits = jnp.where(key < 0, key ^ jnp.int32(0x7FFFFFFF), key)
    bits = jnp.where(_key_valid(u), bits, jnp.int32(0x7F800000))
    return lax.bitcast_convert_type(bits, F32)


def _float_to_key(x):
    bits = lax.bitcast_convert_type(x, jnp.int32)
    key = jnp.where(bits < 0, bits ^ jnp.int32(0x7FFFFFFF), bits)
    return key - KEY_NEG_INF


def _key16_to_float(u16):
    key = u16 + KEY16_NEG_INF
    bits = jnp.where(key < 0, key ^ jnp.int32(0x7FFF), key)
    bits = jnp.where(u16 <= KEY16_SPAN, bits, jnp.int32(0x7F80))
    return lax.bitcast_convert_type(lax.shift_left(bits, 16), F32)


def _attn_body(q_ref, qi_ref, wit_ref, ki_ref, k_ref, vt_ref, tri_ref, o_ref,
               sc_ref, sh_ref, m_ref, l_ref, acc_ref, need_ref, run_ref, bias_ref, lg_ref, p_ref, *, seq, top_k):
    tq, tk = ATT_TQ, ATT_TK
    j = pl.program_id(1)
    q0 = j * tq
    nch = (q0 + tq - 1) // tk + 1
    tpos = q0 + lax.broadcasted_iota(jnp.int32, (1, tq), 1)
    w_scale = (N_IDX_HEADS ** -0.5) * (IDX_DIM ** -0.5)

    def chunk_rows(c):
        return pl.ds(pl.multiple_of(c * tk, tk), tk)

    def key_pos(c):
        return c * tk + lax.broadcasted_iota(jnp.int32, (tk, tq), 0)

    def fold_rows(x):
        groups = tk // SUBLANES
        side = int(math.isqrt(groups))
        x4 = x.reshape(groups // side, side, SUBLANES, tq)
        return jnp.sum(jnp.sum(x4, axis=1), axis=0)

    wsc = wit_ref[...] * w_scale

    def score_chunk(c, carry):
        for rb in range(tk // ATT_RB):
            r0 = pl.multiple_of(c * tk + rb * ATT_RB, ATT_RB)
            kic = ki_ref[pl.ds(r0, ATT_RB), :]
            acc = jnp.zeros((ATT_RB, tq), F32)
            for h in range(N_IDX_HEADS):
                rel = _dot_nt(kic, qi_ref[:, h * IDX_DIM:(h + 1) * IDX_DIM])
                acc = acc + wsc[h:h + 1, :] * jnp.maximum(rel, 0.0)
            kp = r0 + lax.broadcasted_iota(jnp.int32, (ATT_RB, tq), 0)
            masked = jnp.where(kp <= tpos, acc, -jnp.inf)
            sc_ref[pl.ds(r0, ATT_RB), :] = masked
            sh_ref[pl.ds(r0, ATT_RB), :] = masked.astype(BF16)
        return carry

    lax.fori_loop(0, nch, score_chunk, 0)

    def count(pred):
        def body(c, acc):
            return acc + fold_rows(jnp.where(pred(sc_ref[chunk_rows(c), :], c), 1.0, 0.0))
        acc = lax.fori_loop(0, nch, body, jnp.zeros((SUBLANES, tq), F32))
        return jnp.sum(acc, axis=0, keepdims=True)

    def count_rounded(gb):
        pack = 2 * SUBLANES
        one, zero = jnp.ones((), BF16), jnp.zeros((), BF16)

        def body(c, acc):
            ind = jnp.where(sh_ref[chunk_rows(c), :] >= gb, one, zero)
            parts = [ind[r * pack:(r + 1) * pack, :] for r in range(tk // pack)]
            while len(parts) > 1:
                parts = [a + b for a, b in zip(parts[0::2], parts[1::2])]
            return acc + parts[0]
        acc = lax.fori_loop(0, nch, body, jnp.zeros((pack, tq), BF16))
        return jnp.sum(acc.astype(F32), axis=0, keepdims=True)

    kf = float(top_k)

    def coarse_step(it, u16):
        cand = u16 | lax.shift_left(jnp.int32(1), 15 - it)
        cnt = count_rounded(_key16_to_float(cand).astype(BF16))
        return jnp.where((cnt >= kf) & (cand <= KEY16_SPAN), cand, u16)

    u16 = lax.fori_loop(0, 16, coarse_step, jnp.zeros((1, tq), jnp.int32))
    below = jnp.where(u16 == U16_POS_ZERO, U16_POS_ZERO - 2, jnp.maximum(u16 - 1, 0))
    base = _float_to_key(_key16_to_float(below))

    def fine_step(it, carry):
        delta, cnt_u = carry
        cand = delta | lax.shift_left(jnp.int32(1), 16 - it)
        cf = _key_to_float(base + cand)
        cnt = count(lambda s, c: s >= cf)
        ok = (cnt >= kf) & _key_valid(base + cand)
        return jnp.where(ok, cand, delta), jnp.where(ok, cnt, cnt_u)

    base_f = _key_to_float(base)
    cnt_base = count(lambda s, c: s >= base_f)
    delta, cnt_ge = lax.fori_loop(0, 17, fine_step, (jnp.zeros((1, tq), jnp.int32), cnt_base))
    thr = _key_to_float(base + delta)

    tie = (cnt_ge > kf) & (thr > -jnp.inf)
    any_tie = jnp.sum(jnp.where(tie, 1.0, 0.0)) > 0.0

    @pl.when(any_tie)
    def _():
        cnt_gt = count(lambda s, c: s > thr)
        need_ref[...] = jnp.broadcast_to(kf - cnt_gt, need_ref.shape)
        run_ref[...] = jnp.zeros_like(run_ref)

    m_ref[...] = jnp.full(m_ref.shape, NEG_BIG, F32)
    l_ref[...] = jnp.zeros_like(l_ref)
    acc_ref[...] = jnp.zeros_like(acc_ref)
    hpg = N_HEADS // N_KV_HEADS

    rb_rows = ATT_RB
    nrb = tk // rb_rows
    pv_rows = 2 * rb_rows

    def attn_chunks(cs):
        items = [(ci, h) for ci in range(len(cs)) for h in range(N_HEADS)]

        def rows(ci, rb, n=rb_rows):
            return pl.ds(pl.multiple_of(cs[ci] * tk + rb * rb_rows, rb_rows), n)

        def blk(rb, n=rb_rows):
            return slice(rb * rb_rows, rb * rb_rows + n)

        def fold(x, op):
            return op(x.reshape(rb_rows // SUBLANES, SUBLANES, tq), axis=0)

        def qk_block(item, rb):
            ci, h = item
            g = h // hpg
            kc = k_ref[rows(ci, rb), g * HEAD_DIM:(g + 1) * HEAD_DIM]
            return _dot_nt(kc, q_ref[:, h * HEAD_DIM:(h + 1) * HEAD_DIM]) + bias_ref[ci, blk(rb), :]

        def pv_half(item, slot, half, alpha):
            ci, h = item
            g = h // hpg
            cols = pl.ds(pl.multiple_of(cs[ci] * tk + half * pv_rows, pv_rows), pv_rows)
            vtc = vt_ref[g * HEAD_DIM:(g + 1) * HEAD_DIM, cols]
            t = _bdot(vtc, p_ref[slot, blk(half * 2, pv_rows), :])
            if half == 0:
                acc_ref[h] = alpha * acc_ref[h] + t
            else:
                acc_ref[h] = acc_ref[h] + t

        def causal(ci, rb):
            kp = cs[ci] * tk + rb * rb_rows + lax.broadcasted_iota(jnp.int32, (rb_rows, tq), 0)
            return kp <= tpos

        @pl.when(jnp.logical_not(any_tie))
        def _():
            for ci in range(len(cs)):
                for rb in range(nrb):
                    sel = causal(ci, rb) & (sc_ref[rows(ci, rb), :] >= thr)
                    bias_ref[ci, blk(rb), :] = jnp.where(sel, 0.0, NEG_BIG)

        @pl.when(any_tie)
        def _():
            need = need_ref[0:1, :]
            run = run_ref[0:1, :]
            for ci in range(len(cs)):
                for rb in range(nrb):
                    s = sc_ref[rows(ci, rb), :]
                    eq = s == thr
                    eqf = jnp.where(eq, 1.0, 0.0)
                    incl = _bdot(tri_ref[...], eqf.astype(BF16))
                    before = run + (incl - eqf)
                    sel = causal(ci, rb) & ((s > thr) | (eq & (before < need)))
                    bias_ref[ci, blk(rb), :] = jnp.where(sel, 0.0, NEG_BIG)
                    run = run + incl[rb_rows - 1:rb_rows, :]
            run_ref[...] = jnp.broadcast_to(run, run_ref.shape)

        cm = None
        for rb in range(nrb):
            lg = qk_block(items[0], rb)
            lg_ref[0, blk(rb), :] = lg
            part = fold(lg, jnp.max)
            cm = part if cm is None else jnp.maximum(cm, part)

        alpha_prev = None
        for i, (ci, h) in enumerate(items):
            cur = i % 2
            m_prev = m_ref[h]
            m_new = jnp.maximum(m_prev, jnp.max(cm, axis=0, keepdims=True))
            alpha = jnp.exp2(m_prev - m_new)
            m_ref[h] = m_new
            cm = None
            lsum = None
            for rb in range(nrb):
                if i + 1 < len(items):
                    lg = qk_block(items[i + 1], rb)
                    lg_ref[1 - cur, blk(rb), :] = lg
                    part = fold(lg, jnp.max)
                    cm = part if cm is None else jnp.maximum(cm, part)
                pr = jnp.exp2(lg_ref[cur, blk(rb), :] - m_new)
                part = fold(pr, jnp.sum)
                lsum = part if lsum is None else lsum + part
                p_ref[cur, blk(rb), :] = pr.astype(BF16)
                if i > 0 and rb % 2 == 0 and rb // 2 < tk // pv_rows:
                    pv_half(items[i - 1], 1 - cur, rb // 2, alpha_prev)
            l_ref[h] = alpha * l_ref[h] + lsum
            alpha_prev = alpha
        last = len(items) - 1
        for half in range(tk // pv_rows):
            pv_half(items[last], last % 2, half, alpha_prev)

    def attn_pair(i, carry):
        attn_chunks([2 * i, 2 * i + 1])
        return carry

    lax.fori_loop(0, nch // 2, attn_pair, 0)

    @pl.when(nch % 2 == 1)
    def _():
        attn_chunks([nch - 1])

    for h in range(N_HEADS):
        out_t = acc_ref[h] / jnp.sum(l_ref[h], axis=0, keepdims=True)
        o_ref[:, h * HEAD_DIM:(h + 1) * HEAD_DIM] = out_t.T.astype(BF16)


def _attention(q, qi, wit, ki, k, vt, batch, seq):
    n = q.shape[0]
    tq, tk = ATT_TQ, ATT_TK
    nq = seq // tq
    top_k = min(INDEX_TOPK_MAX, seq // 4)
    kvw = N_KV_HEADS * HEAD_DIM
    assert seq % tk == 0 and tk % tq == 0
    assert seq // (2 * SUBLANES) <= 256
    qtile = lambda w: pl.BlockSpec((tq, w), lambda b, j: (b * nq + j, 0))
    whole = lambda w: pl.BlockSpec((seq, w), lambda b, j: (b, 0))
    return pl.pallas_call(
        functools.partial(_attn_body, seq=seq, top_k=top_k),
        out_shape=jax.ShapeDtypeStruct((n, N_HEADS * HEAD_DIM), BF16),
        grid=(batch, nq),
        in_specs=[qtile(N_HEADS * HEAD_DIM), qtile(N_IDX_HEADS * IDX_DIM),
                  pl.BlockSpec((N_IDX_HEADS, tq), lambda b, j: (0, b * nq + j)),
                  whole(IDX_DIM), whole(kvw),
                  pl.BlockSpec((kvw, seq), lambda b, j: (0, b)),
                  pl.BlockSpec((ATT_RB, ATT_RB), lambda b, j: (0, 0))],
        out_specs=qtile(N_HEADS * HEAD_DIM),
        scratch_shapes=[
            pltpu.VMEM((seq, tq), F32),
            pltpu.VMEM((seq, tq), BF16),
            pltpu.VMEM((N_HEADS, 1, tq), F32),
            pltpu.VMEM((N_HEADS, SUBLANES, tq), F32),
            pltpu.VMEM((N_HEADS, HEAD_DIM, tq), F32),
            pltpu.VMEM((SUBLANES, tq), F32),
            pltpu.VMEM((SUBLANES, tq), F32),
            pltpu.VMEM((2, tk, tq), F32),
            pltpu.VMEM((2, tk, tq), F32),
            pltpu.VMEM((2, tk, tq), BF16),
        ],
        compiler_params=_cparams(("arbitrary", "arbitrary")),
        name="dsa_attention",
    )(q, qi, wit, ki, k, vt, jnp.tril(jnp.ones((ATT_RB, ATT_RB), BF16)))


def _merge_body(x_ref, g_ref, wg_ref, ya_ref, yb_ref, oc_ref, wao_ref, wo_ref, o_ref):
    x = x_ref[...]
    u = _rmsnorm_f32(x, g_ref[...]).astype(BF16)
    gt = jax.nn.sigmoid(_bdot(u, wg_ref[...]))
    yc = _bdot(oc_ref[...], wao_ref[...])
    merged = (gt[:, :D_MODEL] * ya_ref[...] + gt[:, D_MODEL:2 * D_MODEL] * yb_ref[...]
              + gt[:, 2 * D_MODEL:] * yc)
    o_ref[...] = x + _bdot(merged.astype(BF16), wo_ref[...])


def _merge(x, l, gain, w_gates, ya, yb, oc, w_attn_out, w_o):
    n = x.shape[0]
    tm = MERGE_TM
    tile = pl.BlockSpec((tm, D_MODEL), lambda i: (i, 0))
    return pl.pallas_call(
        _merge_body,
        out_shape=jax.ShapeDtypeStruct((n, D_MODEL), F32),
        grid=(n // tm,),
        in_specs=[tile, _layer_spec(l, (1, D_MODEL)), _layer_spec(l, (D_MODEL, N_BRANCHES * D_MODEL)),
                  tile, tile, tile,
                  _layer_spec(l, (N_HEADS * HEAD_DIM, D_MODEL)), _layer_spec(l, (D_MODEL, D_MODEL))],
        out_specs=tile,
        compiler_params=_cparams(("arbitrary",)),
        name="merge",
    )(x, gain, w_gates, ya, yb, oc, w_attn_out, w_o)


def _block_diag_groups(w):
    depth = w.shape[0]
    per = GATE_GROUP // RNN_BLOCK_W
    w5 = w.reshape(depth, N_GATE_GROUPS, per, RNN_BLOCK_W, RNN_BLOCK_W)
    eye = jnp.eye(per, dtype=w.dtype)
    return jnp.einsum("lgnij,nm->lgnimj", w5, eye).reshape(depth, N_GATE_GROUPS, GATE_GROUP, GATE_GROUP)


def kernel(x, positions, ffn1_norm, ffn1_w_gate_up, ffn1_w_down, mix_norm, w_in, rnn_conv_w, rnn_conv_b,
           rnn_gate_a_w, rnn_gate_a_b, rnn_gate_x_w, rnn_gate_x_b, rnn_lambda, rnn_w_out, sconv_w,
           sconv_w_out, attn_w_out, w_o, ffn2_norm, ffn2_w_gate_up, ffn2_w_down, final_norm):
    batch, seq, d = x.shape
    depth = w_in.shape[0]
    n = batch * seq
    xf = x.reshape(n, d)
    rows = lambda v: v.reshape(depth, 1, -1)
    bf = lambda w: w.astype(BF16)

    tabs = _rope_tables(positions.reshape(n, 1))

    o_v = 2 * D_RNN + 3 * D_SCONV + (N_HEADS + N_KV_HEADS) * HEAD_DIM
    o_qi = o_v + N_KV_HEADS * HEAD_DIM
    o_wi = o_qi + N_IDX_HEADS * IDX_DIM + IDX_DIM
    o_gates = o_wi + N_IDX_HEADS

    w_in_b = bf(w_in)
    w_gates = bf(w_in[:, :, o_gates:])
    w_t = bf(jnp.swapaxes(jnp.concatenate([w_in[:, :, o_v:o_qi], w_in[:, :, o_wi:o_gates]], axis=2), 1, 2))
    wbd = bf(jnp.concatenate([_block_diag_groups(rnn_gate_a_w), _block_diag_groups(rnn_gate_x_w)], axis=-1))
    ffn1_gu, ffn1_d, ffn2_gu, ffn2_d = bf(ffn1_w_gate_up), bf(ffn1_w_down), bf(ffn2_w_gate_up), bf(ffn2_w_down)
    rnn_wo, sconv_wo, attn_wo, w_o_b = bf(rnn_w_out), bf(sconv_w_out), bf(attn_w_out), bf(w_o)
    g_ffn1, g_mix, g_ffn2 = rows(ffn1_norm), rows(mix_norm), rows(ffn2_norm)
    conv_b, gate_a_b, gate_x_b, lam = rows(rnn_conv_b), rows(rnn_gate_a_b), rows(rnn_gate_x_b), rows(rnn_lambda)
    g_final = final_norm.reshape(1, -1)

    for l in range(depth):
        xf = _ffn(xf, l, g_ffn1, ffn1_gu, ffn1_d, g_final, False)
        ya = _rnn_branch(xf, l, g_mix, w_in_b, rnn_conv_w, conv_b, wbd, gate_a_b, gate_x_b, lam, rnn_wo, seq)
        yb = _sconv_branch(xf, l, g_mix, w_in_b, sconv_w, sconv_wo, seq)
        q, k, qi, ki, vt, wit = _qkv_proj(xf, l, g_mix, w_in_b, w_t, tabs)
        oc = _attention(q, qi, wit, ki, k, vt, batch, seq)
        xf = _merge(xf, l, g_mix, w_gates, ya, yb, oc, attn_wo, w_o_b)
        xf = _ffn(xf, l, g_ffn2, ffn2_gu, ffn2_d, g_final, l == depth - 1)

    return xf.reshape(batch, seq, d)
```

```python
import functools
import math

import jax
import jax.numpy as jnp
import numpy as np
from jax import lax
from jax.experimental import pallas as pl
from jax.experimental.pallas import tpu as pltpu

F32 = jnp.float32
BF16 = jnp.bfloat16

D_MODEL = 1024
D_FF = 2816
D_RNN = 1024
RNN_BLOCKS = 16
RNN_BLOCK_W = D_RNN // RNN_BLOCKS
RNN_CONV_W = 4
RG_LRU_C = 8.0
D_SCONV = 1024
SCONV_W = 3
N_HEADS = 8
N_KV_HEADS = 2
HEAD_DIM = 128
N_IDX_HEADS = 8
IDX_DIM = 64
INDEX_TOPK_MAX = 256
ROPE_THETA = 500000.0
ROPE_FRACTION_DEN = 4
N_BRANCHES = 3
RMS_EPS = 1e-6

LANES = 128
SUBLANES = 8
VMEM_LIMIT = 56 * 1024 * 1024

FFN_TM = 1024
FFN_TF = 256
RNN_TM = 256
SCONV_TM = 512
QKV_TM = 512
MERGE_TM = 512
TAB_TM = 1024
ATT_TQ = 256
ATT_TK = 512
ATT_RB = 128
ATT_GROUP = 4

GATE_GROUP = 256
N_GATE_GROUPS = D_RNN // GATE_GROUP

NEG_BIG = -1e30
LOGIT_SCALE = (HEAD_DIM ** -0.5) * math.log2(math.e)
KEY_NEG_INF = np.int32(np.uint32(0x807FFFFF).view(np.int32))
KEY_SPAN = 0x7F800000 - int(KEY_NEG_INF)
KEY16_NEG_INF = np.int32(np.uint16(0x807F).view(np.int16))
KEY16_SPAN = 0x7F80 - int(KEY16_NEG_INF)
U16_POS_ZERO = -int(KEY16_NEG_INF)


def _cparams(sem, flags=None):
    return pltpu.CompilerParams(dimension_semantics=sem, vmem_limit_bytes=VMEM_LIMIT, flags=flags)


def _layer_spec(l, shape, col=0):
    zeros = (0,) * (len(shape) - 1)
    return pl.BlockSpec((None,) + tuple(shape), lambda *_: (l,) + zeros + (col,))


def _rmsnorm_f32(x, g):
    var = jnp.mean(x * x, axis=-1, keepdims=True)
    return (x * lax.rsqrt(var + RMS_EPS)) * g


def _bdot(a, b):
    return jnp.dot(a, b, preferred_element_type=F32)


def _dot_nt(a, b):
    return lax.dot_general(a, b, (((1,), (1,)), ((), ())), preferred_element_type=F32)


def _ffn_body(x_ref, g_ref, wg_ref, wu_ref, wd_ref, fn_ref, o_ref, xn_ref, acc_ref, *, n_ff, final_norm):
    j = pl.program_id(1)

    @pl.when(j == 0)
    def _():
        xn_ref[...] = _rmsnorm_f32(x_ref[...], g_ref[...]).astype(BF16)
        acc_ref[...] = jnp.zeros_like(acc_ref)

    xn = xn_ref[...]
    g = _bdot(xn, wg_ref[...])
    u = _bdot(xn, wu_ref[...])
    a = (g * jax.nn.sigmoid(g) * u).astype(BF16)
    acc_ref[...] += _bdot(a, wd_ref[...])

    @pl.when(j == n_ff - 1)
    def _():
        y = x_ref[...] + 0.5 * acc_ref[...]
        if final_norm:
            y = _rmsnorm_f32(y, fn_ref[...])
        o_ref[...] = y


def _ffn(x, l, gain, w_gu, w_d, fnorm, final_norm):
    n = x.shape[0]
    n_ff = D_FF // FFN_TF
    grid = (n // FFN_TM, n_ff)
    return pl.pallas_call(
        functools.partial(_ffn_body, n_ff=n_ff, final_norm=final_norm),
        out_shape=jax.ShapeDtypeStruct((n, D_MODEL), F32),
        grid=grid,
        in_specs=[
            pl.BlockSpec((FFN_TM, D_MODEL), lambda i, j: (i, 0)),
            pl.BlockSpec((None, 1, D_MODEL), lambda i, j: (l, 0, 0)),
            pl.BlockSpec((None, D_MODEL, FFN_TF), lambda i, j: (l, 0, j)),
            pl.BlockSpec((None, D_MODEL, FFN_TF), lambda i, j: (l, 0, j + n_ff)),
            pl.BlockSpec((None, FFN_TF, D_MODEL), lambda i, j: (l, j, 0)),
            pl.BlockSpec((1, D_MODEL), lambda i, j: (0, 0)),
        ],
        out_specs=pl.BlockSpec((FFN_TM, D_MODEL), lambda i, j: (i, 0)),
        scratch_shapes=[pltpu.VMEM((FFN_TM, D_MODEL), BF16), pltpu.VMEM((FFN_TM, D_MODEL), F32)],
        compiler_params=_cparams(("arbitrary", "arbitrary")),
        name="ffn",
    )(x, gain, w_gu, w_gu, w_d, fnorm)


def _causal_dwconv_tile(x, prev_rows, w_ref, width):
    tm, ch = x.shape
    groups = tm // SUBLANES
    x3 = x.reshape(groups, SUBLANES, ch)
    sub = lax.broadcasted_iota(jnp.int32, x3.shape, 1)
    out = x3 * w_ref[width - 1:width, :]
    for lag in range(1, width):
        cur = pltpu.roll(x3, lag, 1)
        head = pltpu.roll(prev_rows.reshape(1, SUBLANES, ch), lag, 1)
        prev = jnp.concatenate([head, cur[:groups - 1]], axis=0)
        out = out + jnp.where(sub >= lag, cur, prev) * w_ref[width - 1 - lag:width - lag, :]
    return out.reshape(tm, ch)


def _rnn_body(x_ref, g_ref, w_ref, cw_ref, cb_ref, wbd_ref, ba_ref, bx_ref, lam_ref, wout_ref,
              o_ref, xbuf_ref, h_ref, *, tiles_per_seq):
    tm = x_ref.shape[0]
    i = pl.program_id(0)

    @pl.when(i % tiles_per_seq == 0)
    def _():
        xbuf_ref[...] = jnp.zeros_like(xbuf_ref)
        h_ref[...] = jnp.zeros_like(h_ref)

    u = _rmsnorm_f32(x_ref[...], g_ref[...]).astype(BF16)
    p = _bdot(u, w_ref[...])
    rx = p[:, :D_RNN]
    rg = p[:, D_RNN:]

    xa = cb_ref[...] + _causal_dwconv_tile(rx, xbuf_ref[...], cw_ref, RNN_CONV_W)
    xbuf_ref[...] = rx[tm - SUBLANES:, :]

    xab = xa.astype(BF16)
    r_parts, i_parts = [], []
    for jg in range(N_GATE_GROUPS):
        gj = _bdot(xab[:, jg * GATE_GROUP:(jg + 1) * GATE_GROUP], wbd_ref[jg])
        r_parts.append(gj[:, :GATE_GROUP])
        i_parts.append(gj[:, GATE_GROUP:])
    r = jax.nn.sigmoid(jnp.concatenate(r_parts, axis=1) + ba_ref[...])
    ig = jax.nn.sigmoid(jnp.concatenate(i_parts, axis=1) + bx_ref[...])

    nl = -lam_ref[...]
    sp = jnp.maximum(nl, 0.0) + jnp.log1p(jnp.exp(-jnp.abs(nl)))
    log_a = (-RG_LRU_C) * r * sp
    a = jnp.exp(log_a)
    one_m_a2 = -jnp.tanh(log_a) * (a * a + 1.0)
    in_scale = jnp.where(one_m_a2 > 0.0, one_m_a2 * lax.rsqrt(one_m_a2), 0.0)
    b = in_scale * (ig * xa)

    groups = tm // SUBLANES
    a3 = a.reshape(groups, SUBLANES, D_RNN)
    b3 = b.reshape(groups, SUBLANES, D_RNN)
    sub = lax.broadcasted_iota(jnp.int32, a3.shape, 1)
    d = 1
    while d < SUBLANES:
        keep = sub >= d
        a_s = jnp.where(keep, pltpu.roll(a3, d, 1), 1.0)
        b_s = jnp.where(keep, pltpu.roll(b3, d, 1), 0.0)
        b3 = a3 * b_s + b3
        a3 = a3 * a_s
        d *= 2
    carry = h_ref[...]
    h_groups = []
    for gi in range(groups):
        hg = b3[gi] + a3[gi] * carry
        h_groups.append(hg)
        carry = hg[SUBLANES - 1:SUBLANES, :]
    h_ref[...] = carry
    h = jnp.concatenate(h_groups, axis=0)

    gl = 0.5 * rg * (1.0 + jnp.tanh(math.sqrt(2.0 / math.pi) * (rg + 0.044715 * (rg * rg * rg))))
    o_ref[...] = _bdot((h * gl).astype(BF16), wout_ref[...])


def _rnn_branch(x, l, gain, w_in, conv_w, conv_b, wbd, ba, bx, lam, w_out, seq):
    n = x.shape[0]
    tm = RNN_TM
    return pl.pallas_call(
        functools.partial(_rnn_body, tiles_per_seq=seq // tm),
        out_shape=jax.ShapeDtypeStruct((n, D_MODEL), F32),
        grid=(n // tm,),
        in_specs=[
            pl.BlockSpec((tm, D_MODEL), lambda i: (i, 0)),
            _layer_spec(l, (1, D_MODEL)),
            _layer_spec(l, (D_MODEL, 2 * D_RNN)),
            _layer_spec(l, (RNN_CONV_W, D_RNN)),
            _layer_spec(l, (1, D_RNN)),
            _layer_spec(l, (N_GATE_GROUPS, GATE_GROUP, 2 * GATE_GROUP)),
            _layer_spec(l, (1, D_RNN)),
            _layer_spec(l, (1, D_RNN)),
            _layer_spec(l, (1, D_RNN)),
            _layer_spec(l, (D_RNN, D_MODEL)),
        ],
        out_specs=pl.BlockSpec((tm, D_MODEL), lambda i: (i, 0)),
        scratch_shapes=[pltpu.VMEM((SUBLANES, D_RNN), F32), pltpu.VMEM((1, D_RNN), F32)],
        compiler_params=_cparams(("arbitrary",)),
        name="rnn_branch",
    )(x, gain, w_in, conv_w, conv_b, wbd, ba, bx, lam, w_out)


def _sconv_body(x_ref, g_ref, wb_ref, wc_ref, wh_ref, sw_ref, wout_ref, o_ref, zbuf_ref, *, tiles_per_seq):
    tm = x_ref.shape[0]
    i = pl.program_id(0)

    @pl.when(i % tiles_per_seq == 0)
    def _():
        zbuf_ref[...] = jnp.zeros_like(zbuf_ref)

    u = _rmsnorm_f32(x_ref[...], g_ref[...]).astype(BF16)
    c_b = _bdot(u, wb_ref[...])
    z = _bdot(u, wc_ref[...]) * _bdot(u, wh_ref[...])
    cv = _causal_dwconv_tile(z, zbuf_ref[...], sw_ref, SCONV_W)
    zbuf_ref[...] = z[tm - SUBLANES:, :]
    o_ref[...] = _bdot((c_b * cv).astype(BF16), wout_ref[...])


def _sconv_branch(x, l, gain, w_in, sw, w_out, seq):
    n = x.shape[0]
    tm = SCONV_TM
    col0 = 2 * D_RNN // D_SCONV
    return pl.pallas_call(
        functools.partial(_sconv_body, tiles_per_seq=seq // tm),
        out_shape=jax.ShapeDtypeStruct((n, D_MODEL), F32),
        grid=(n // tm,),
        in_specs=[
            pl.BlockSpec((tm, D_MODEL), lambda i: (i, 0)),
            _layer_spec(l, (1, D_MODEL)),
            _layer_spec(l, (D_MODEL, D_SCONV), col0),
            _layer_spec(l, (D_MODEL, D_SCONV), col0 + 1),
            _layer_spec(l, (D_MODEL, D_SCONV), col0 + 2),
            _layer_spec(l, (SCONV_W, D_SCONV)),
            _layer_spec(l, (D_SCONV, D_MODEL)),
        ],
        out_specs=pl.BlockSpec((tm, D_MODEL), lambda i: (i, 0)),
        scratch_shapes=[pltpu.VMEM((SUBLANES, D_SCONV), F32)],
        compiler_params=_cparams(("arbitrary",)),
        name="sconv_branch",
    )(x, gain, w_in, w_in, w_in, sw, w_out)


def _tab_body(pos_ref, f128_ref, s128_ref, f64_ref, s64_ref, c128_ref, sg128_ref, c64_ref, sg64_ref):
    posf = pos_ref[...].astype(F32)
    for f_ref, s_ref, c_out, s_out in ((f128_ref, s128_ref, c128_ref, sg128_ref),
                                       (f64_ref, s64_ref, c64_ref, sg64_ref)):
        sgn = s_ref[...]
        ang = posf * f_ref[...]
        c_out[...] = jnp.where(sgn != 0.0, jnp.cos(ang), 1.0)
        s_out[...] = jnp.sin(ang) * sgn


def _rope_lane_rows(period, rot):
    half = rot // 2
    inv_freq = ROPE_THETA ** (-jnp.arange(0, rot, 2, dtype=F32) / rot)
    lane = np.arange(LANES) % period
    f_row = jnp.where(lane < rot, inv_freq[lane % half], 0.0).astype(F32)
    s_row = np.where(lane < half, -1.0, np.where(lane < rot, 1.0, 0.0)).astype(np.float32)
    return f_row.reshape(1, LANES), jnp.asarray(s_row).reshape(1, LANES)


def _rope_tables(pos_col):
    n = pos_col.shape[0]
    f128, s128 = _rope_lane_rows(HEAD_DIM, HEAD_DIM // ROPE_FRACTION_DEN)
    f64, s64 = _rope_lane_rows(IDX_DIM, IDX_DIM // ROPE_FRACTION_DEN)
    row = pl.BlockSpec((1, LANES), lambda i: (0, 0))
    tab = pl.BlockSpec((TAB_TM, LANES), lambda i: (i, 0))
    return pl.pallas_call(
        _tab_body,
        out_shape=[jax.ShapeDtypeStruct((n, LANES), F32)] * 4,
        grid=(n // TAB_TM,),
        in_specs=[pl.BlockSpec((TAB_TM, 1), lambda i: (i, 0)), row, row, row, row],
        out_specs=[tab] * 4,
        compiler_params=_cparams(("arbitrary",)),
        name="rope_tables",
    )(pos_col, f128, s128, f64, s64)


def _rope_apply(xh, c, sg, first_half, half):
    partner = jnp.where(first_half, pltpu.roll(xh, LANES - half, 1), pltpu.roll(xh, half, 1))
    return xh * c + partner * sg


def _qkv_body(x_ref, g_ref, wq_ref, wk_ref, wqi_ref, wki_ref, wt_ref, c128_ref, s128_ref, c64_ref, s64_ref,
              q_ref, k_ref, qi_ref, ki_ref, vt_ref, wit_ref):
    tm = x_ref.shape[0]
    u = _rmsnorm_f32(x_ref[...], g_ref[...]).astype(BF16)
    p = jnp.concatenate([_bdot(u, w[...]) for w in (wq_ref, wk_ref, wqi_ref, wki_ref)], axis=1)
    pt = _dot_nt(wt_ref[...], u)
    kvw = N_KV_HEADS * HEAD_DIM
    vt_ref[...] = pt[:kvw, :].astype(BF16)
    wit_ref[...] = pt[kvw:kvw + N_IDX_HEADS, :]
    lane = lax.broadcasted_iota(jnp.int32, (tm, LANES), 1)
    h128 = HEAD_DIM // ROPE_FRACTION_DEN // 2
    h64 = IDX_DIM // ROPE_FRACTION_DEN // 2
    first128 = lane < h128
    first64 = (lane % IDX_DIM) < h64
    c128, s128 = c128_ref[...], s128_ref[...]
    c64, s64 = c64_ref[...], s64_ref[...]

    off = 0
    for h in range(N_HEADS):
        blk = p[:, off + h * LANES: off + (h + 1) * LANES]
        roped = _rope_apply(blk, c128, s128, first128, h128)
        q_ref[:, h * LANES:(h + 1) * LANES] = (roped * LOGIT_SCALE).astype(BF16)
    off += N_HEADS * HEAD_DIM
    for h in range(N_KV_HEADS):
        blk = p[:, off + h * LANES: off + (h + 1) * LANES]
        k_ref[:, h * LANES:(h + 1) * LANES] = _rope_apply(blk, c128, s128, first128, h128).astype(BF16)
    off += N_KV_HEADS * HEAD_DIM
    for h in range(N_IDX_HEADS * IDX_DIM // LANES):
        blk = p[:, off + h * LANES: off + (h + 1) * LANES]
        qi_ref[:, h * LANES:(h + 1) * LANES] = _rope_apply(blk, c64, s64, first64, h64).astype(BF16)
    off += N_IDX_HEADS * IDX_DIM
    blk = p[:, off: off + LANES]
    ki_ref[...] = _rope_apply(blk, c64, s64, first64, h64)[:, :IDX_DIM].astype(BF16)


def _qkv_proj(x, l, gain, w_in, w_t, tabs):
    n = x.shape[0]
    tm = QKV_TM
    qw, kw, qiw = N_HEADS * HEAD_DIM, N_KV_HEADS * HEAD_DIM, N_IDX_HEADS * IDX_DIM
    o_q = 2 * D_RNN + 3 * D_SCONV
    o_k = o_q + qw
    o_qi = o_k + 2 * kw
    o_ki = o_qi + qiw
    tile = lambda w: pl.BlockSpec((tm, w), lambda i: (i, 0))
    ttile = lambda r: pl.BlockSpec((r, tm), lambda i: (0, i))
    kvw = N_KV_HEADS * HEAD_DIM
    return pl.pallas_call(
        _qkv_body,
        out_shape=[
            jax.ShapeDtypeStruct((n, N_HEADS * HEAD_DIM), BF16),
            jax.ShapeDtypeStruct((n, kvw), BF16),
            jax.ShapeDtypeStruct((n, N_IDX_HEADS * IDX_DIM), BF16),
            jax.ShapeDtypeStruct((n, IDX_DIM), BF16),
            jax.ShapeDtypeStruct((kvw, n), BF16),
            jax.ShapeDtypeStruct((N_IDX_HEADS, n), F32),
        ],
        grid=(n // tm,),
        in_specs=[tile(D_MODEL), _layer_spec(l, (1, D_MODEL)),
                  _layer_spec(l, (D_MODEL, qw), o_q // qw),
                  _layer_spec(l, (D_MODEL, kw), o_k // kw),
                  _layer_spec(l, (D_MODEL, qiw), o_qi // qiw),
                  _layer_spec(l, (D_MODEL, LANES), o_ki // LANES),
                  _layer_spec(l, w_t.shape[1:]),
                  tile(LANES), tile(LANES), tile(LANES), tile(LANES)],
        out_specs=[tile(N_HEADS * HEAD_DIM), tile(kvw), tile(N_IDX_HEADS * IDX_DIM), tile(IDX_DIM),
                   ttile(kvw), ttile(N_IDX_HEADS)],
        compiler_params=_cparams(("arbitrary",)),
        name="qkv_proj",
    )(x, gain, w_in, w_in, w_in, w_in, w_t, *tabs)


def _key_valid(u):
    return (u ^ jnp.int32(-2 ** 31)) <= jnp.int32(KEY_SPAN - 2 ** 31)


def _key_to_float(u):
    key = u + KEY_NEG_INF
    bits = jnp.where(key < 0, key ^ jnp.int32(0x7FFFFFFF), key)
    bits = jnp.where(_key_valid(u), bits, jnp.int32(0x7F800000))
    return lax.bitcast_convert_type(bits, F32)


def _float_to_key(x):
    bits = lax.bitcast_convert_type(x, jnp.int32)
    key = jnp.where(bits < 0, bits ^ jnp.int32(0x7FFFFFFF), bits)
    return key - KEY_NEG_INF


def _key16_to_float(u16):
    key = u16 + KEY16_NEG_INF
    bits = jnp.where(key < 0, key ^ jnp.int32(0x7FFF), key)
    bits = jnp.where(u16 <= KEY16_SPAN, bits, jnp.int32(0x7F80))
    return lax.bitcast_convert_type(lax.shift_left(bits, 16), F32)


def _attn_body(q_ref, qi_ref, wit_ref, ki_ref, k_ref, vt_ref, tri_ref, o_ref,
               sc_ref, sh_ref, m_ref, l_ref, acc_ref, need_ref, run_ref, bias_ref, lg_ref, p_ref, *, seq, top_k):
    tq, tk = ATT_TQ, ATT_TK
    j = pl.program_id(1)
    q0 = j * tq
    nch = (q0 + tq - 1) // tk + 1
    tpos = q0 + lax.broadcasted_iota(jnp.int32, (1, tq), 1)
    w_scale = (N_IDX_HEADS ** -0.5) * (IDX_DIM ** -0.5)

    def chunk_rows(c):
        return pl.ds(pl.multiple_of(c * tk, tk), tk)

    def key_pos(c):
        return c * tk + lax.broadcasted_iota(jnp.int32, (tk, tq), 0)

    def fold_rows(x):
        groups = tk // SUBLANES
        side = int(math.isqrt(groups))
        x4 = x.reshape(groups // side, side, SUBLANES, tq)
        return jnp.sum(jnp.sum(x4, axis=1), axis=0)

    wsc = wit_ref[...] * w_scale

    def score_chunk(c, carry):
        for rb in range(tk // ATT_RB):
            r0 = pl.multiple_of(c * tk + rb * ATT_RB, ATT_RB)
            kic = ki_ref[pl.ds(r0, ATT_RB), :]
            acc = jnp.zeros((ATT_RB, tq), F32)
            for h in range(N_IDX_HEADS):
                rel = _dot_nt(kic, qi_ref[:, h * IDX_DIM:(h + 1) * IDX_DIM])
                acc = acc + wsc[h:h + 1, :] * jnp.maximum(rel, 0.0)
            kp = r0 + lax.broadcasted_iota(jnp.int32, (ATT_RB, tq), 0)
            masked = jnp.where(kp <= tpos, acc, -jnp.inf)
            sc_ref[pl.ds(r0, ATT_RB), :] = masked
            sh_ref[pl.ds(r0, ATT_RB), :] = masked.astype(BF16)
        return carry

    lax.fori_loop(0, nch, score_chunk, 0)

    def count(pred):
        def body(c, acc):
            return acc + fold_rows(jnp.where(pred(sc_ref[chunk_rows(c), :], c), 1.0, 0.0))
        acc = lax.fori_loop(0, nch, body, jnp.zeros((SUBLANES, tq), F32))
        return jnp.sum(acc, axis=0, keepdims=True)

    def count_rounded(gb):
        pack = 2 * SUBLANES
        one, zero = jnp.ones((), BF16), jnp.zeros((), BF16)

        def body(c, acc):
            ind = jnp.where(sh_ref[chunk_rows(c), :] >= gb, one, zero)
            parts = [ind[r * pack:(r + 1) * pack, :] for r in range(tk // pack)]
            while len(parts) > 1:
                parts = [a + b for a, b in zip(parts[0::2], parts[1::2])]
            return acc + parts[0]
        acc = lax.fori_loop(0, nch, body, jnp.zeros((pack, tq), BF16))
        return jnp.sum(acc.astype(F32), axis=0, keepdims=True)

    kf = float(top_k)

    def coarse_step(it, u16):
        cand = u16 | lax.shift_left(jnp.int32(1), 15 - it)
        cnt = count_rounded(_key16_to_float(cand).astype(BF16))
        return jnp.where((cnt >= kf) & (cand <= KEY16_SPAN), cand, u16)

    u16 = lax.fori_loop(0, 16, coarse_step, jnp.zeros((1, tq), jnp.int32))
    below = jnp.where(u16 == U16_POS_ZERO, U16_POS_ZERO - 2, jnp.maximum(u16 - 1, 0))
    base = _float_to_key(_key16_to_float(below))

    def fine_step(it, carry):
        delta, cnt_u = carry
        cand = delta | lax.shift_left(jnp.int32(1), 16 - it)
        cf = _key_to_float(base + cand)
        cnt = count(lambda s, c: s >= cf)
        ok = (cnt >= kf) & _key_valid(base + cand)
        return jnp.where(ok, cand, delta), jnp.where(ok, cnt, cnt_u)

    n_scored = jnp.full((1, tq), nch * tk, jnp.int32).astype(F32)
    delta, cnt_ge = lax.fori_loop(0, 17, fine_step, (jnp.zeros((1, tq), jnp.int32), n_scored))
    thr = _key_to_float(base + delta)

    tie = (cnt_ge > kf) & (thr > -jnp.inf)
    any_tie = jnp.sum(jnp.where(tie, 1.0, 0.0)) > 0.0

    @pl.when(any_tie)
    def _():
        cnt_gt = count(lambda s, c: s > thr)
        need_ref[...] = jnp.broadcast_to(kf - cnt_gt, need_ref.shape)
        run_ref[...] = jnp.zeros_like(run_ref)

    m_ref[...] = jnp.full(m_ref.shape, NEG_BIG, F32)
    l_ref[...] = jnp.zeros_like(l_ref)
    acc_ref[...] = jnp.zeros_like(acc_ref)
    hpg = N_HEADS // N_KV_HEADS

    rb_rows = ATT_RB
    nrb = tk // rb_rows
    pv_rows = 2 * rb_rows

    def attn_chunks(cs):
        items = [(ci, h) for ci in range(len(cs)) for h in range(N_HEADS)]

        def rows(ci, rb, n=rb_rows):
            return pl.ds(pl.multiple_of(cs[ci] * tk + rb * rb_rows, rb_rows), n)

        def blk(rb, n=rb_rows):
            return slice(rb * rb_rows, rb * rb_rows + n)

        def fold(x, op):
            return op(x.reshape(rb_rows // SUBLANES, SUBLANES, tq), axis=0)

        def qk_block(item, rb):
            ci, h = item
            g = h // hpg
            kc = k_ref[rows(ci, rb), g * HEAD_DIM:(g + 1) * HEAD_DIM]
            return _dot_nt(kc, q_ref[:, h * HEAD_DIM:(h + 1) * HEAD_DIM]) + bias_ref[ci, blk(rb), :]

        def pv_half(item, slot, half, alpha):
            ci, h = item
            g = h // hpg
            cols = pl.ds(pl.multiple_of(cs[ci] * tk + half * pv_rows, pv_rows), pv_rows)
            vtc = vt_ref[g * HEAD_DIM:(g + 1) * HEAD_DIM, cols]
            t = _bdot(vtc, p_ref[slot, blk(half * 2, pv_rows), :])
            if half == 0:
                acc_ref[h] = alpha * acc_ref[h] + t
            else:
                acc_ref[h] = acc_ref[h] + t

        def causal(ci, rb):
            kp = cs[ci] * tk + rb * rb_rows + lax.broadcasted_iota(jnp.int32, (rb_rows, tq), 0)
            return kp <= tpos

        @pl.when(jnp.logical_not(any_tie))
        def _():
            for ci in range(len(cs)):
                for rb in range(nrb):
                    sel = causal(ci, rb) & (sc_ref[rows(ci, rb), :] >= thr)
                    bias_ref[ci, blk(rb), :] = jnp.where(sel, 0.0, NEG_BIG)

        @pl.when(any_tie)
        def _():
            need = need_ref[0:1, :]
            run = run_ref[0:1, :]
            for ci in range(len(cs)):
                for rb in range(nrb):
                    s = sc_ref[rows(ci, rb), :]
                    eq = s == thr
                    eqf = jnp.where(eq, 1.0, 0.0)
                    incl = _bdot(tri_ref[...], eqf.astype(BF16))
                    before = run + (incl - eqf)
                    sel = causal(ci, rb) & ((s > thr) | (eq & (before < need)))
                    bias_ref[ci, blk(rb), :] = jnp.where(sel, 0.0, NEG_BIG)
                    run = run + incl[rb_rows - 1:rb_rows, :]
            run_ref[...] = jnp.broadcast_to(run, run_ref.shape)

        cm = None
        for rb in range(nrb):
            lg = qk_block(items[0], rb)
            lg_ref[0, blk(rb), :] = lg
            part = fold(lg, jnp.max)
            cm = part if cm is None else jnp.maximum(cm, part)

        alpha_prev = None
        for i, (ci, h) in enumerate(items):
            cur = i % 2
            m_prev = m_ref[h]
            m_new = jnp.maximum(m_prev, jnp.max(cm, axis=0, keepdims=True))
            alpha = jnp.exp2(m_prev - m_new)
            m_ref[h] = m_new
            cm = None
            lsum = None
            for rb in range(nrb):
                if i + 1 < len(items):
                    lg = qk_block(items[i + 1], rb)
                    lg_ref[1 - cur, blk(rb), :] = lg
                    part = fold(lg, jnp.max)
                    cm = part if cm is None else jnp.maximum(cm, part)
                pr = jnp.exp2(lg_ref[cur, blk(rb), :] - m_new)
                part = fold(pr, jnp.sum)
                lsum = part if lsum is None else lsum + part
                p_ref[cur, blk(rb), :] = pr.astype(BF16)
                if i > 0 and rb % 2 == 0 and rb // 2 < tk // pv_rows:
                    pv_half(items[i - 1], 1 - cur, rb // 2, alpha_prev)
            l_ref[h] = alpha * l_ref[h] + lsum
            alpha_prev = alpha
        last = len(items) - 1
        for half in range(tk // pv_rows):
            pv_half(items[last], last % 2, half, alpha_prev)

    def attn_group(i, carry):
        attn_chunks([ATT_GROUP * i + t for t in range(ATT_GROUP)])
        return carry

    n_groups = nch // ATT_GROUP
    lax.fori_loop(0, n_groups, attn_group, 0)
    done = n_groups * ATT_GROUP
    size = ATT_GROUP // 2
    while size >= 1:
        start = done

        @pl.when((nch - start) >= size)
        def _(start=start, size=size):
            attn_chunks([start + t for t in range(size)])

        done = jnp.where((nch - start) >= size, start + size, start)
        size //= 2

    for h in range(N_HEADS):
        out_t = acc_ref[h] / jnp.sum(l_ref[h], axis=0, keepdims=True)
        o_ref[:, h * HEAD_DIM:(h + 1) * HEAD_DIM] = out_t.T.astype(BF16)


def _attention(q, qi, wit, ki, k, vt, batch, seq):
    n = q.shape[0]
    tq, tk = ATT_TQ, ATT_TK
    nq = seq // tq
    top_k = min(INDEX_TOPK_MAX, seq // 4)
    kvw = N_KV_HEADS * HEAD_DIM
    assert seq % tk == 0 and tk % tq == 0
    assert seq // (2 * SUBLANES) <= 256
    qtile = lambda w: pl.BlockSpec((tq, w), lambda b, j: (b * nq + j, 0))
    whole = lambda w: pl.BlockSpec((seq, w), lambda b, j: (b, 0))
    return pl.pallas_call(
        functools.partial(_attn_body, seq=seq, top_k=top_k),
        out_shape=jax.ShapeDtypeStruct((n, N_HEADS * HEAD_DIM), BF16),
        grid=(batch, nq),
        in_specs=[qtile(N_HEADS * HEAD_DIM), qtile(N_IDX_HEADS * IDX_DIM),
                  pl.BlockSpec((N_IDX_HEADS, tq), lambda b, j: (0, b * nq + j)),
                  whole(IDX_DIM), whole(kvw),
                  pl.BlockSpec((kvw, seq), lambda b, j: (0, b)),
                  pl.BlockSpec((ATT_RB, ATT_RB), lambda b, j: (0, 0))],
        out_specs=qtile(N_HEADS * HEAD_DIM),
        scratch_shapes=[
            pltpu.VMEM((seq, tq), F32),
            pltpu.VMEM((seq, tq), BF16),
            pltpu.VMEM((N_HEADS, 1, tq), F32),
            pltpu.VMEM((N_HEADS, SUBLANES, tq), F32),
            pltpu.VMEM((N_HEADS, HEAD_DIM, tq), F32),
            pltpu.VMEM((SUBLANES, tq), F32),
            pltpu.VMEM((SUBLANES, tq), F32),
            pltpu.VMEM((ATT_GROUP, tk, tq), F32),
            pltpu.VMEM((2, tk, tq), F32),
            pltpu.VMEM((2, tk, tq), BF16),
        ],
        compiler_params=_cparams(("arbitrary", "arbitrary")),
        name="dsa_attention",
    )(q, qi, wit, ki, k, vt, jnp.tril(jnp.ones((ATT_RB, ATT_RB), BF16)))


def _merge_body(x_ref, g_ref, wg_ref, ya_ref, yb_ref, oc_ref, wao_ref, wo_ref, o_ref):
    x = x_ref[...]
    u = _rmsnorm_f32(x, g_ref[...]).astype(BF16)
    gt = jax.nn.sigmoid(_bdot(u, wg_ref[...]))
    yc = _bdot(oc_ref[...], wao_ref[...])
    merged = (gt[:, :D_MODEL] * ya_ref[...] + gt[:, D_MODEL:2 * D_MODEL] * yb_ref[...]
              + gt[:, 2 * D_MODEL:] * yc)
    o_ref[...] = x + _bdot(merged.astype(BF16), wo_ref[...])


def _merge(x, l, gain, w_gates, ya, yb, oc, w_attn_out, w_o):
    n = x.shape[0]
    tm = MERGE_TM
    tile = pl.BlockSpec((tm, D_MODEL), lambda i: (i, 0))
    return pl.pallas_call(
        _merge_body,
        out_shape=jax.ShapeDtypeStruct((n, D_MODEL), F32),
        grid=(n // tm,),
        in_specs=[tile, _layer_spec(l, (1, D_MODEL)), _layer_spec(l, (D_MODEL, N_BRANCHES * D_MODEL)),
                  tile, tile, tile,
                  _layer_spec(l, (N_HEADS * HEAD_DIM, D_MODEL)), _layer_spec(l, (D_MODEL, D_MODEL))],
        out_specs=tile,
        compiler_params=_cparams(("arbitrary",)),
        name="merge",
    )(x, gain, w_gates, ya, yb, oc, w_attn_out, w_o)


def _block_diag_groups(w):
    depth = w.shape[0]
    per = GATE_GROUP // RNN_BLOCK_W
    w5 = w.reshape(depth, N_GATE_GROUPS, per, RNN_BLOCK_W, RNN_BLOCK_W)
    eye = jnp.eye(per, dtype=w.dtype)
    return jnp.einsum("lgnij,nm->lgnimj", w5, eye).reshape(depth, N_GATE_GROUPS, GATE_GROUP, GATE_GROUP)


def kernel(x, positions, ffn1_norm, ffn1_w_gate_up, ffn1_w_down, mix_norm, w_in, rnn_conv_w, rnn_conv_b,
           rnn_gate_a_w, rnn_gate_a_b, rnn_gate_x_w, rnn_gate_x_b, rnn_lambda, rnn_w_out, sconv_w,
           sconv_w_out, attn_w_out, w_o, ffn2_norm, ffn2_w_gate_up, ffn2_w_down, final_norm):
    batch, seq, d = x.shape
    depth = w_in.shape[0]
    n = batch * seq
    xf = x.reshape(n, d)
    rows = lambda v: v.reshape(depth, 1, -1)
    bf = lambda w: w.astype(BF16)

    tabs = _rope_tables(positions.reshape(n, 1))

    o_v = 2 * D_RNN + 3 * D_SCONV + (N_HEADS + N_KV_HEADS) * HEAD_DIM
    o_qi = o_v + N_KV_HEADS * HEAD_DIM
    o_wi = o_qi + N_IDX_HEADS * IDX_DIM + IDX_DIM
    o_gates = o_wi + N_IDX_HEADS

    w_in_b = bf(w_in)
    w_gates = bf(w_in[:, :, o_gates:])
    w_t = bf(jnp.swapaxes(jnp.concatenate([w_in[:, :, o_v:o_qi], w_in[:, :, o_wi:o_gates]], axis=2), 1, 2))
    wbd = bf(jnp.concatenate([_block_diag_groups(rnn_gate_a_w), _block_diag_groups(rnn_gate_x_w)], axis=-1))
    ffn1_gu, ffn1_d, ffn2_gu, ffn2_d = bf(ffn1_w_gate_up), bf(ffn1_w_down), bf(ffn2_w_gate_up), bf(ffn2_w_down)
    rnn_wo, sconv_wo, attn_wo, w_o_b = bf(rnn_w_out), bf(sconv_w_out), bf(attn_w_out), bf(w_o)
    g_ffn1, g_mix, g_ffn2 = rows(ffn1_norm), rows(mix_norm), rows(ffn2_norm)
    conv_b, gate_a_b, gate_x_b, lam = rows(rnn_conv_b), rows(rnn_gate_a_b), rows(rnn_gate_x_b), rows(rnn_lambda)
    g_final = final_norm.reshape(1, -1)

    for l in range(depth):
        xf = _ffn(xf, l, g_ffn1, ffn1_gu, ffn1_d, g_final, False)
        ya = _rnn_branch(xf, l, g_mix, w_in_b, rnn_conv_w, conv_b, wbd, gate_a_b, gate_x_b, lam, rnn_wo, seq)
        yb = _sconv_branch(xf, l, g_mix, w_in_b, sconv_w, sconv_wo, seq)
        q, k, qi, ki, vt, wit = _qkv_proj(xf, l, g_mix, w_in_b, w_t, tabs)
        oc = _attention(q, qi, wit, ki, k, vt, batch, seq)
        xf = _merge(xf, l, g_mix, w_gates, ya, yb, oc, attn_wo, w_o_b)
        xf = _ffn(xf, l, g_ffn2, ffn2_gu, ffn2_d, g_final, l == depth - 1)

    return xf.reshape(batch, seq, d)
```

```python
import functools
import math

import jax
import jax.numpy as jnp
import numpy as np
from jax import lax
from jax.experimental import pallas as pl
from jax.experimental.pallas import tpu as pltpu

F32 = jnp.float32
BF16 = jnp.bfloat16

D_MODEL = 1024
D_FF = 2816
D_RNN = 1024
RNN_BLOCKS = 16
RNN_BLOCK_W = D_RNN // RNN_BLOCKS
RNN_CONV_W = 4
RG_LRU_C = 8.0
D_SCONV = 1024
SCONV_W = 3
N_HEADS = 8
N_KV_HEADS = 2
HEAD_DIM = 128
N_IDX_HEADS = 8
IDX_DIM = 64
INDEX_TOPK_MAX = 256
ROPE_THETA = 500000.0
ROPE_FRACTION_DEN = 4
N_BRANCHES = 3
RMS_EPS = 1e-6

LANES = 128
SUBLANES = 8
VMEM_LIMIT = 56 * 1024 * 1024

FFN_TM = 1024
FFN_TF = 256
RNN_TM = 256
SCONV_TM = 512
QKV_TM = 512
MERGE_TM = 512
TAB_TM = 1024
ATT_TQ = 256
ATT_TK = 512
ATT_RB = 128
ATT_GROUP = 4

GATE_GROUP = 256
N_GATE_GROUPS = D_RNN // GATE_GROUP

NEG_BIG = -1e30
LOGIT_SCALE = (HEAD_DIM ** -0.5) * math.log2(math.e)
KEY_NEG_INF = np.int32(np.uint32(0x807FFFFF).view(np.int32))
KEY_SPAN = 0x7F800000 - int(KEY_NEG_INF)
KEY16_NEG_INF = np.int32(np.uint16(0x807F).view(np.int16))
KEY16_SPAN = 0x7F80 - int(KEY16_NEG_INF)
U16_POS_ZERO = -int(KEY16_NEG_INF)


def _cparams(sem, flags=None):
    return pltpu.CompilerParams(dimension_semantics=sem, vmem_limit_bytes=VMEM_LIMIT, flags=flags)


def _layer_spec(l, shape, col=0):
    zeros = (0,) * (len(shape) - 1)
    return pl.BlockSpec((None,) + tuple(shape), lambda *_: (l,) + zeros + (col,))


def _rmsnorm_f32(x, g):
    var = jnp.mean(x * x, axis=-1, keepdims=True)
    return (x * lax.rsqrt(var + RMS_EPS)) * g


def _bdot(a, b):
    return jnp.dot(a, b, preferred_element_type=F32)


def _dot_nt(a, b):
    return lax.dot_general(a, b, (((1,), (1,)), ((), ())), preferred_element_type=F32)


def _ffn_body(x_ref, g_ref, wg_ref, wu_ref, wd_ref, fn_ref, o_ref, xn_ref, acc_ref, *, n_ff, final_norm):
    j = pl.program_id(1)

    @pl.when(j == 0)
    def _():
        xn_ref[...] = _rmsnorm_f32(x_ref[...], g_ref[...]).astype(BF16)
        acc_ref[...] = jnp.zeros_like(acc_ref)

    xn = xn_ref[...]
    g = _bdot(xn, wg_ref[...])
    u = _bdot(xn, wu_ref[...])
    a = (g * jax.nn.sigmoid(g) * u).astype(BF16)
    acc_ref[...] += _bdot(a, wd_ref[...])

    @pl.when(j == n_ff - 1)
    def _():
        y = x_ref[...] + 0.5 * acc_ref[...]
        if final_norm:
            y = _rmsnorm_f32(y, fn_ref[...])
        o_ref[...] = y


def _ffn(x, l, gain, w_gu, w_d, fnorm, final_norm):
    n = x.shape[0]
    n_ff = D_FF // FFN_TF
    grid = (n // FFN_TM, n_ff)
    return pl.pallas_call(
        functools.partial(_ffn_body, n_ff=n_ff, final_norm=final_norm),
        out_shape=jax.ShapeDtypeStruct((n, D_MODEL), F32),
        grid=grid,
        in_specs=[
            pl.BlockSpec((FFN_TM, D_MODEL), lambda i, j: (i, 0)),
            pl.BlockSpec((None, 1, D_MODEL), lambda i, j: (l, 0, 0)),
            pl.BlockSpec((None, D_MODEL, FFN_TF), lambda i, j: (l, 0, j)),
            pl.BlockSpec((None, D_MODEL, FFN_TF), lambda i, j: (l, 0, j + n_ff)),
            pl.BlockSpec((None, FFN_TF, D_MODEL), lambda i, j: (l, j, 0)),
            pl.BlockSpec((1, D_MODEL), lambda i, j: (0, 0)),
        ],
        out_specs=pl.BlockSpec((FFN_TM, D_MODEL), lambda i, j: (i, 0)),
        scratch_shapes=[pltpu.VMEM((FFN_TM, D_MODEL), BF16), pltpu.VMEM((FFN_TM, D_MODEL), F32)],
        compiler_params=_cparams(("arbitrary", "arbitrary")),
        name="ffn",
    )(x, gain, w_gu, w_gu, w_d, fnorm)


def _causal_dwconv_tile(x, prev_rows, w_ref, width):
    tm, ch = x.shape
    groups = tm // SUBLANES
    x3 = x.reshape(groups, SUBLANES, ch)
    sub = lax.broadcasted_iota(jnp.int32, x3.shape, 1)
    out = x3 * w_ref[width - 1:width, :]
    for lag in range(1, width):
        cur = pltpu.roll(x3, lag, 1)
        head = pltpu.roll(prev_rows.reshape(1, SUBLANES, ch), lag, 1)
        prev = jnp.concatenate([head, cur[:groups - 1]], axis=0)
        out = out + jnp.where(sub >= lag, cur, prev) * w_ref[width - 1 - lag:width - lag, :]
    return out.reshape(tm, ch)


def _rnn_body(x_ref, g_ref, w_ref, cw_ref, cb_ref, wbd_ref, ba_ref, bx_ref, lam_ref, wout_ref,
              o_ref, xbuf_ref, h_ref, *, tiles_per_seq):
    tm = x_ref.shape[0]
    i = pl.program_id(0)

    @pl.when(i % tiles_per_seq == 0)
    def _():
        xbuf_ref[...] = jnp.zeros_like(xbuf_ref)
        h_ref[...] = jnp.zeros_like(h_ref)

    u = _rmsnorm_f32(x_ref[...], g_ref[...]).astype(BF16)
    p = _bdot(u, w_ref[...])
    rx = p[:, :D_RNN]
    rg = p[:, D_RNN:]

    xa = cb_ref[...] + _causal_dwconv_tile(rx, xbuf_ref[...], cw_ref, RNN_CONV_W)
    xbuf_ref[...] = rx[tm - SUBLANES:, :]

    xab = xa.astype(BF16)
    r_parts, i_parts = [], []
    for jg in range(N_GATE_GROUPS):
        gj = _bdot(xab[:, jg * GATE_GROUP:(jg + 1) * GATE_GROUP], wbd_ref[jg])
        r_parts.append(gj[:, :GATE_GROUP])
        i_parts.append(gj[:, GATE_GROUP:])
    r = jax.nn.sigmoid(jnp.concatenate(r_parts, axis=1) + ba_ref[...])
    ig = jax.nn.sigmoid(jnp.concatenate(i_parts, axis=1) + bx_ref[...])

    nl = -lam_ref[...]
    sp = jnp.maximum(nl, 0.0) + jnp.log1p(jnp.exp(-jnp.abs(nl)))
    log_a = (-RG_LRU_C) * r * sp
    a = jnp.exp(log_a)
    one_m_a2 = -jnp.tanh(log_a) * (a * a + 1.0)
    in_scale = jnp.where(one_m_a2 > 0.0, one_m_a2 * lax.rsqrt(one_m_a2), 0.0)
    b = in_scale * (ig * xa)

    groups = tm // SUBLANES
    a3 = a.reshape(groups, SUBLANES, D_RNN)
    b3 = b.reshape(groups, SUBLANES, D_RNN)
    sub = lax.broadcasted_iota(jnp.int32, a3.shape, 1)
    d = 1
    while d < SUBLANES:
        keep = sub >= d
        a_s = jnp.where(keep, pltpu.roll(a3, d, 1), 1.0)
        b_s = jnp.where(keep, pltpu.roll(b3, d, 1), 0.0)
        b3 = a3 * b_s + b3
        a3 = a3 * a_s
        d *= 2
    carry = h_ref[...]
    h_groups = []
    for gi in range(groups):
        hg = b3[gi] + a3[gi] * carry
        h_groups.append(hg)
        carry = hg[SUBLANES - 1:SUBLANES, :]
    h_ref[...] = carry
    h = jnp.concatenate(h_groups, axis=0)

    gl = 0.5 * rg * (1.0 + jnp.tanh(math.sqrt(2.0 / math.pi) * (rg + 0.044715 * (rg * rg * rg))))
    o_ref[...] = _bdot((h * gl).astype(BF16), wout_ref[...])


def _rnn_branch(x, l, gain, w_in, conv_w, conv_b, wbd, ba, bx, lam, w_out, seq):
    n = x.shape[0]
    tm = RNN_TM
    return pl.pallas_call(
        functools.partial(_rnn_body, tiles_per_seq=seq // tm),
        out_shape=jax.ShapeDtypeStruct((n, D_MODEL), F32),
        grid=(n // tm,),
        in_specs=[
            pl.BlockSpec((tm, D_MODEL), lambda i: (i, 0)),
            _layer_spec(l, (1, D_MODEL)),
            _layer_spec(l, (D_MODEL, 2 * D_RNN)),
            _layer_spec(l, (RNN_CONV_W, D_RNN)),
            _layer_spec(l, (1, D_RNN)),
            _layer_spec(l, (N_GATE_GROUPS, GATE_GROUP, 2 * GATE_GROUP)),
            _layer_spec(l, (1, D_RNN)),
            _layer_spec(l, (1, D_RNN)),
            _layer_spec(l, (1, D_RNN)),
            _layer_spec(l, (D_RNN, D_MODEL)),
        ],
        out_specs=pl.BlockSpec((tm, D_MODEL), lambda i: (i, 0)),
        scratch_shapes=[pltpu.VMEM((SUBLANES, D_RNN), F32), pltpu.VMEM((1, D_RNN), F32)],
        compiler_params=_cparams(("arbitrary",)),
        name="rnn_branch",
    )(x, gain, w_in, conv_w, conv_b, wbd, ba, bx, lam, w_out)


def _sconv_body(x_ref, g_ref, wb_ref, wc_ref, wh_ref, sw_ref, wout_ref, o_ref, zbuf_ref, *, tiles_per_seq):
    tm = x_ref.shape[0]
    i = pl.program_id(0)

    @pl.when(i % tiles_per_seq == 0)
    def _():
        zbuf_ref[...] = jnp.zeros_like(zbuf_ref)

    u = _rmsnorm_f32(x_ref[...], g_ref[...]).astype(BF16)
    c_b = _bdot(u, wb_ref[...])
    z = _bdot(u, wc_ref[...]) * _bdot(u, wh_ref[...])
    cv = _causal_dwconv_tile(z, zbuf_ref[...], sw_ref, SCONV_W)
    zbuf_ref[...] = z[tm - SUBLANES:, :]
    o_ref[...] = _bdot((c_b * cv).astype(BF16), wout_ref[...])


def _sconv_branch(x, l, gain, w_in, sw, w_out, seq):
    n = x.shape[0]
    tm = SCONV_TM
    col0 = 2 * D_RNN // D_SCONV
    return pl.pallas_call(
        functools.partial(_sconv_body, tiles_per_seq=seq // tm),
        out_shape=jax.ShapeDtypeStruct((n, D_MODEL), F32),
        grid=(n // tm,),
        in_specs=[
            pl.BlockSpec((tm, D_MODEL), lambda i: (i, 0)),
            _layer_spec(l, (1, D_MODEL)),
            _layer_spec(l, (D_MODEL, D_SCONV), col0),
            _layer_spec(l, (D_MODEL, D_SCONV), col0 + 1),
            _layer_spec(l, (D_MODEL, D_SCONV), col0 + 2),
            _layer_spec(l, (SCONV_W, D_SCONV)),
            _layer_spec(l, (D_SCONV, D_MODEL)),
        ],
        out_specs=pl.BlockSpec((tm, D_MODEL), lambda i: (i, 0)),
        scratch_shapes=[pltpu.VMEM((SUBLANES, D_SCONV), F32)],
        compiler_params=_cparams(("arbitrary",)),
        name="sconv_branch",
    )(x, gain, w_in, w_in, w_in, sw, w_out)


def _tab_body(pos_ref, f128_ref, s128_ref, f64_ref, s64_ref, c128_ref, sg128_ref, c64_ref, sg64_ref):
    posf = pos_ref[...].astype(F32)
    lane = lax.broadcasted_iota(jnp.int32, (1, LANES), 1)
    upper = lane >= IDX_DIM
    ang = posf * jnp.where(upper, f64_ref[...], f128_ref[...])
    cos, sin = jnp.cos(ang), jnp.sin(ang)
    s128, s64 = s128_ref[...], s64_ref[...]
    c128_ref[...] = jnp.where(s128 != 0.0, cos, 1.0)
    sg128_ref[...] = sin * s128
    cos64 = jnp.where(upper, cos, pltpu.roll(cos, IDX_DIM, 1))
    sin64 = jnp.where(upper, sin, pltpu.roll(sin, IDX_DIM, 1))
    c64_ref[...] = jnp.where(s64 != 0.0, cos64, 1.0)
    sg64_ref[...] = sin64 * s64


def _rope_lane_rows(period, rot):
    half = rot // 2
    inv_freq = ROPE_THETA ** (-jnp.arange(0, rot, 2, dtype=F32) / rot)
    lane = np.arange(LANES) % period
    f_row = jnp.where(lane < rot, inv_freq[lane % half], 0.0).astype(F32)
    s_row = np.where(lane < half, -1.0, np.where(lane < rot, 1.0, 0.0)).astype(np.float32)
    return f_row.reshape(1, LANES), jnp.asarray(s_row).reshape(1, LANES)


def _rope_tables(pos_col):
    n = pos_col.shape[0]
    f128, s128 = _rope_lane_rows(HEAD_DIM, HEAD_DIM // ROPE_FRACTION_DEN)
    f64, s64 = _rope_lane_rows(IDX_DIM, IDX_DIM // ROPE_FRACTION_DEN)
    row = pl.BlockSpec((1, LANES), lambda i: (0, 0))
    tab = pl.BlockSpec((TAB_TM, LANES), lambda i: (i, 0))
    return pl.pallas_call(
        _tab_body,
        out_shape=[jax.ShapeDtypeStruct((n, LANES), F32)] * 4,
        grid=(n // TAB_TM,),
        in_specs=[pl.BlockSpec((TAB_TM, 1), lambda i: (i, 0)), row, row, row, row],
        out_specs=[tab] * 4,
        compiler_params=_cparams(("arbitrary",)),
        name="rope_tables",
    )(pos_col, f128, s128, f64, s64)


def _rope_apply(xh, c, sg, first_half, half):
    partner = jnp.where(first_half, pltpu.roll(xh, LANES - half, 1), pltpu.roll(xh, half, 1))
    return xh * c + partner * sg


def _qkv_body(x_ref, g_ref, wq_ref, wk_ref, wqi_ref, wki_ref, wt_ref, c128_ref, s128_ref, c64_ref, s64_ref,
              q_ref, k_ref, qi_ref, ki_ref, vt_ref, wit_ref):
    tm = x_ref.shape[0]
    u = _rmsnorm_f32(x_ref[...], g_ref[...]).astype(BF16)
    p = jnp.concatenate([_bdot(u, w[...]) for w in (wq_ref, wk_ref, wqi_ref, wki_ref)], axis=1)
    pt = _dot_nt(wt_ref[...], u)
    kvw = N_KV_HEADS * HEAD_DIM
    vt_ref[...] = pt[:kvw, :].astype(BF16)
    wit_ref[...] = pt[kvw:kvw + N_IDX_HEADS, :]
    lane = lax.broadcasted_iota(jnp.int32, (tm, LANES), 1)
    h128 = HEAD_DIM // ROPE_FRACTION_DEN // 2
    h64 = IDX_DIM // ROPE_FRACTION_DEN // 2
    first128 = lane < h128
    first64 = (lane % IDX_DIM) < h64
    c128, s128 = c128_ref[...], s128_ref[...]
    c64, s64 = c64_ref[...], s64_ref[...]

    off = 0
    for h in range(N_HEADS):
        blk = p[:, off + h * LANES: off + (h + 1) * LANES]
        roped = _rope_apply(blk, c128, s128, first128, h128)
        q_ref[:, h * LANES:(h + 1) * LANES] = (roped * LOGIT_SCALE).astype(BF16)
    off += N_HEADS * HEAD_DIM
    for h in range(N_KV_HEADS):
        blk = p[:, off + h * LANES: off + (h + 1) * LANES]
        k_ref[:, h * LANES:(h + 1) * LANES] = _rope_apply(blk, c128, s128, first128, h128).astype(BF16)
    off += N_KV_HEADS * HEAD_DIM
    for h in range(N_IDX_HEADS * IDX_DIM // LANES):
        blk = p[:, off + h * LANES: off + (h + 1) * LANES]
        qi_ref[:, h * LANES:(h + 1) * LANES] = _rope_apply(blk, c64, s64, first64, h64).astype(BF16)
    off += N_IDX_HEADS * IDX_DIM
    blk = p[:, off: off + LANES]
    ki_ref[...] = _rope_apply(blk, c64, s64, first64, h64)[:, :IDX_DIM].astype(BF16)


def _qkv_proj(x, l, gain, w_in, w_t, tabs):
    n = x.shape[0]
    tm = QKV_TM
    qw, kw, qiw = N_HEADS * HEAD_DIM, N_KV_HEADS * HEAD_DIM, N_IDX_HEADS * IDX_DIM
    o_q = 2 * D_RNN + 3 * D_SCONV
    o_k = o_q + qw
    o_qi = o_k + 2 * kw
    o_ki = o_qi + qiw
    tile = lambda w: pl.BlockSpec((tm, w), lambda i: (i, 0))
    ttile = lambda r: pl.BlockSpec((r, tm), lambda i: (0, i))
    kvw = N_KV_HEADS * HEAD_DIM
    return pl.pallas_call(
        _qkv_body,
        out_shape=[
            jax.ShapeDtypeStruct((n, N_HEADS * HEAD_DIM), BF16),
            jax.ShapeDtypeStruct((n, kvw), BF16),
            jax.ShapeDtypeStruct((n, N_IDX_HEADS * IDX_DIM), BF16),
            jax.ShapeDtypeStruct((n, IDX_DIM), BF16),
            jax.ShapeDtypeStruct((kvw, n), BF16),
            jax.ShapeDtypeStruct((N_IDX_HEADS, n), F32),
        ],
        grid=(n // tm,),
        in_specs=[tile(D_MODEL), _layer_spec(l, (1, D_MODEL)),
                  _layer_spec(l, (D_MODEL, qw), o_q // qw),
                  _layer_spec(l, (D_MODEL, kw), o_k // kw),
                  _layer_spec(l, (D_MODEL, qiw), o_qi // qiw),
                  _layer_spec(l, (D_MODEL, LANES), o_ki // LANES),
                  _layer_spec(l, w_t.shape[1:]),
                  tile(LANES), tile(LANES), tile(LANES), tile(LANES)],
        out_specs=[tile(N_HEADS * HEAD_DIM), tile(kvw), tile(N_IDX_HEADS * IDX_DIM), tile(IDX_DIM),
                   ttile(kvw), ttile(N_IDX_HEADS)],
        compiler_params=_cparams(("arbitrary",)),
        name="qkv_proj",
    )(x, gain, w_in, w_in, w_in, w_in, w_t, *tabs)


def _key_valid(u):
    return (u ^ jnp.int32(-2 ** 31)) <= jnp.int32(KEY_SPAN - 2 ** 31)


def _key_to_float(u):
    key = u + KEY_NEG_INF
    bits = jnp.where(key < 0, key ^ jnp.int32(0x7FFFFFFF), key)
    bits = jnp.where(_key_valid(u), bits, jnp.int32(0x7F800000))
    return lax.bitcast_convert_type(bits, F32)


def _float_to_key(x):
    bits = lax.bitcast_convert_type(x, jnp.int32)
    key = jnp.where(bits < 0, bits ^ jnp.int32(0x7FFFFFFF), bits)
    return key - KEY_NEG_INF


def _key16_to_float(u16):
    key = u16 + KEY16_NEG_INF
    bits = jnp.where(key < 0, key ^ jnp.int32(0x7FFF), key)
    bits = jnp.where(u16 <= KEY16_SPAN, bits, jnp.int32(0x7F80))
    return lax.bitcast_convert_type(lax.shift_left(bits, 16), F32)


def _attn_body(q_ref, qi_ref, wit_ref, ki_ref, k_ref, vt_ref, tri_ref, o_ref,
               sc_ref, sh_ref, m_ref, l_ref, acc_ref, need_ref, run_ref, bias_ref, lg_ref, p_ref, *, seq, top_k):
    tq, tk = ATT_TQ, ATT_TK
    j = pl.program_id(1)
    q0 = j * tq
    nch = (q0 + tq - 1) // tk + 1
    tpos = q0 + lax.broadcasted_iota(jnp.int32, (1, tq), 1)
    w_scale = (N_IDX_HEADS ** -0.5) * (IDX_DIM ** -0.5)

    def chunk_rows(c):
        return pl.ds(pl.multiple_of(c * tk, tk), tk)

    def key_pos(c):
        return c * tk + lax.broadcasted_iota(jnp.int32, (tk, tq), 0)

    def fold_rows(x):
        groups = tk // SUBLANES
        side = int(math.isqrt(groups))
        x4 = x.reshape(groups // side, side, SUBLANES, tq)
        return jnp.sum(jnp.sum(x4, axis=1), axis=0)

    wsc = wit_ref[...] * w_scale

    def score_chunk(c, carry):
        for rb in range(tk // ATT_RB):
            r0 = pl.multiple_of(c * tk + rb * ATT_RB, ATT_RB)
            kic = ki_ref[pl.ds(r0, ATT_RB), :]
            acc = jnp.zeros((ATT_RB, tq), F32)
            for h in range(N_IDX_HEADS):
                rel = _dot_nt(kic, qi_ref[:, h * IDX_DIM:(h + 1) * IDX_DIM])
                acc = acc + wsc[h:h + 1, :] * jnp.maximum(rel, 0.0)
            kp = r0 + lax.broadcasted_iota(jnp.int32, (ATT_RB, tq), 0)
            masked = jnp.where(kp <= tpos, acc, -jnp.inf)
            sc_ref[pl.ds(r0, ATT_RB), :] = masked
            sh_ref[pl.ds(r0, ATT_RB), :] = masked.astype(BF16)
        return carry

    lax.fori_loop(0, nch, score_chunk, 0)

    def count(pred):
        def body(c, acc):
            return acc + fold_rows(jnp.where(pred(sc_ref[chunk_rows(c), :], c), 1.0, 0.0))
        acc = lax.fori_loop(0, nch, body, jnp.zeros((SUBLANES, tq), F32))
        return jnp.sum(acc, axis=0, keepdims=True)

    def count_rounded(gb):
        pack = 2 * SUBLANES
        one, zero = jnp.ones((), BF16), jnp.zeros((), BF16)

        def body(c, acc):
            ind = jnp.where(sh_ref[chunk_rows(c), :] >= gb, one, zero)
            parts = [ind[r * pack:(r + 1) * pack, :] for r in range(tk // pack)]
            while len(parts) > 1:
                parts = [a + b for a, b in zip(parts[0::2], parts[1::2])]
            return acc + parts[0]
        acc = lax.fori_loop(0, nch, body, jnp.zeros((pack, tq), BF16))
        return jnp.sum(acc.astype(F32), axis=0, keepdims=True)

    kf = float(top_k)

    def coarse_step(it, u16):
        cand = u16 | lax.shift_left(jnp.int32(1), 15 - it)
        cnt = count_rounded(_key16_to_float(cand).astype(BF16))
        return jnp.where((cnt >= kf) & (cand <= KEY16_SPAN), cand, u16)

    u16 = lax.fori_loop(0, 16, coarse_step, jnp.zeros((1, tq), jnp.int32))
    below = jnp.where(u16 == U16_POS_ZERO, U16_POS_ZERO - 2, jnp.maximum(u16 - 1, 0))
    base = _float_to_key(_key16_to_float(below))

    def fine_step(it, carry):
        delta, cnt_u = carry
        cand = delta | lax.shift_left(jnp.int32(1), 16 - it)
        cf = _key_to_float(base + cand)
        cnt = count(lambda s, c: s >= cf)
        ok = (cnt >= kf) & _key_valid(base + cand)
        return jnp.where(ok, cand, delta), jnp.where(ok, cnt, cnt_u)

    n_scored = jnp.full((1, tq), nch * tk, jnp.int32).astype(F32)
    delta, cnt_ge = lax.fori_loop(0, 17, fine_step, (jnp.zeros((1, tq), jnp.int32), n_scored))
    thr = _key_to_float(base + delta)

    tie = (cnt_ge > kf) & (thr > -jnp.inf)
    any_tie = jnp.sum(jnp.where(tie, 1.0, 0.0)) > 0.0

    @pl.when(any_tie)
    def _():
        cnt_gt = count(lambda s, c: s > thr)
        need_ref[...] = jnp.broadcast_to(kf - cnt_gt, need_ref.shape)
        run_ref[...] = jnp.zeros_like(run_ref)

    m_ref[...] = jnp.full(m_ref.shape, NEG_BIG, F32)
    l_ref[...] = jnp.zeros_like(l_ref)
    acc_ref[...] = jnp.zeros_like(acc_ref)
    hpg = N_HEADS // N_KV_HEADS

    rb_rows = ATT_RB
    nrb = tk // rb_rows
    pv_rows = 2 * rb_rows

    def attn_chunks(cs):
        items = [(ci, h) for ci in range(len(cs)) for h in range(N_HEADS)]

        def rows(ci, rb, n=rb_rows):
            return pl.ds(pl.multiple_of(cs[ci] * tk + rb * rb_rows, rb_rows), n)

        def blk(rb, n=rb_rows):
            return slice(rb * rb_rows, rb * rb_rows + n)

        def fold(x, op):
            return op(x.reshape(rb_rows // SUBLANES, SUBLANES, tq), axis=0)

        def qk_block(item, rb):
            ci, h = item
            g = h // hpg
            kc = k_ref[rows(ci, rb), g * HEAD_DIM:(g + 1) * HEAD_DIM]
            return _dot_nt(kc, q_ref[:, h * HEAD_DIM:(h + 1) * HEAD_DIM]) + bias_ref[ci, blk(rb), :]

        def pv_half(item, slot, half, alpha):
            ci, h = item
            g = h // hpg
            cols = pl.ds(pl.multiple_of(cs[ci] * tk + half * pv_rows, pv_rows), pv_rows)
            vtc = vt_ref[g * HEAD_DIM:(g + 1) * HEAD_DIM, cols]
            t = _bdot(vtc, p_ref[slot, blk(half * 2, pv_rows), :])
            if half == 0:
                acc_ref[h] = alpha * acc_ref[h] + t
            else:
                acc_ref[h] = acc_ref[h] + t

        def causal(ci, rb):
            kp = cs[ci] * tk + rb * rb_rows + lax.broadcasted_iota(jnp.int32, (rb_rows, tq), 0)
            return kp <= tpos

        @pl.when(jnp.logical_not(any_tie))
        def _():
            for ci in range(len(cs)):
                for rb in range(nrb):
                    sel = causal(ci, rb) & (sc_ref[rows(ci, rb), :] >= thr)
                    bias_ref[ci, blk(rb), :] = jnp.where(sel, 0.0, NEG_BIG)

        @pl.when(any_tie)
        def _():
            need = need_ref[0:1, :]
            run = run_ref[0:1, :]
            for ci in range(len(cs)):
                for rb in range(nrb):
                    s = sc_ref[rows(ci, rb), :]
                    eq = s == thr
                    eqf = jnp.where(eq, 1.0, 0.0)
                    incl = _bdot(tri_ref[...], eqf.astype(BF16))
                    before = run + (incl - eqf)
                    sel = causal(ci, rb) & ((s > thr) | (eq & (before < need)))
                    bias_ref[ci, blk(rb), :] = jnp.where(sel, 0.0, NEG_BIG)
                    run = run + incl[rb_rows - 1:rb_rows, :]
            run_ref[...] = jnp.broadcast_to(run, run_ref.shape)

        cm = None
        for rb in range(nrb):
            lg = qk_block(items[0], rb)
            lg_ref[0, blk(rb), :] = lg
            part = fold(lg, jnp.max)
            cm = part if cm is None else jnp.maximum(cm, part)

        alpha_prev = None
        for i, (ci, h) in enumerate(items):
            cur = i % 2
            m_prev = m_ref[h]
            m_new = jnp.maximum(m_prev, jnp.max(cm, axis=0, keepdims=True))
            alpha = jnp.exp2(m_prev - m_new)
            m_ref[h] = m_new
            cm = None
            lsum = None
            for rb in range(nrb):
                if i + 1 < len(items):
                    lg = qk_block(items[i + 1], rb)
                    lg_ref[1 - cur, blk(rb), :] = lg
                    part = fold(lg, jnp.max)
                    cm = part if cm is None else jnp.maximum(cm, part)
                pr = jnp.exp2(lg_ref[cur, blk(rb), :] - m_new)
                part = fold(pr, jnp.sum)
                lsum = part if lsum is None else lsum + part
                p_ref[cur, blk(rb), :] = pr.astype(BF16)
                if i > 0 and rb % 2 == 0 and rb // 2 < tk // pv_rows:
                    pv_half(items[i - 1], 1 - cur, rb // 2, alpha_prev)
            l_ref[h] = alpha * l_ref[h] + lsum
            alpha_prev = alpha
        last = len(items) - 1
        for half in range(tk // pv_rows):
            pv_half(items[last], last % 2, half, alpha_prev)

    def attn_group(i, carry):
        attn_chunks([ATT_GROUP * i + t for t in range(ATT_GROUP)])
        return carry

    n_groups = nch // ATT_GROUP
    lax.fori_loop(0, n_groups, attn_group, 0)
    done = n_groups * ATT_GROUP
    size = ATT_GROUP // 2
    while size >= 1:
        start = done

        @pl.when((nch - start) >= size)
        def _(start=start, size=size):
            attn_chunks([start + t for t in range(size)])

        done = jnp.where((nch - start) >= size, start + size, start)
        size //= 2

    for h in range(N_HEADS):
        out_t = acc_ref[h] / jnp.sum(l_ref[h], axis=0, keepdims=True)
        o_ref[:, h * HEAD_DIM:(h + 1) * HEAD_DIM] = out_t.T.astype(BF16)


def _attention(q, qi, wit, ki, k, vt, batch, seq):
    n = q.shape[0]
    tq, tk = ATT_TQ, ATT_TK
    nq = seq // tq
    top_k = min(INDEX_TOPK_MAX, seq // 4)
    kvw = N_KV_HEADS * HEAD_DIM
    assert seq % tk == 0 and tk % tq == 0
    assert seq // (2 * SUBLANES) <= 256
    qtile = lambda w: pl.BlockSpec((tq, w), lambda b, j: (b * nq + j, 0))
    whole = lambda w: pl.BlockSpec((seq, w), lambda b, j: (b, 0))
    return pl.pallas_call(
        functools.partial(_attn_body, seq=seq, top_k=top_k),
        out_shape=jax.ShapeDtypeStruct((n, N_HEADS * HEAD_DIM), BF16),
        grid=(batch, nq),
        in_specs=[qtile(N_HEADS * HEAD_DIM), qtile(N_IDX_HEADS * IDX_DIM),
                  pl.BlockSpec((N_IDX_HEADS, tq), lambda b, j: (0, b * nq + j)),
                  whole(IDX_DIM), whole(kvw),
                  pl.BlockSpec((kvw, seq), lambda b, j: (0, b)),
                  pl.BlockSpec((ATT_RB, ATT_RB), lambda b, j: (0, 0))],
        out_specs=qtile(N_HEADS * HEAD_DIM),
        scratch_shapes=[
            pltpu.VMEM((seq, tq), F32),
            pltpu.VMEM((seq, tq), BF16),
            pltpu.VMEM((N_HEADS, 1, tq), F32),
            pltpu.VMEM((N_HEADS, SUBLANES, tq), F32),
            pltpu.VMEM((N_HEADS, HEAD_DIM, tq), F32),
            pltpu.VMEM((SUBLANES, tq), F32),
            pltpu.VMEM((SUBLANES, tq), F32),
            pltpu.VMEM((ATT_GROUP, tk, tq), F32),
            pltpu.VMEM((2, tk, tq), F32),
            pltpu.VMEM((2, tk, tq), BF16),
        ],
        compiler_params=_cparams(("arbitrary", "arbitrary")),
        name="dsa_attention",
    )(q, qi, wit, ki, k, vt, jnp.tril(jnp.ones((ATT_RB, ATT_RB), BF16)))


def _merge_body(x_ref, g_ref, wg_ref, ya_ref, yb_ref, oc_ref, wao_ref, wo_ref, o_ref):
    x = x_ref[...]
    u = _rmsnorm_f32(x, g_ref[...]).astype(BF16)
    gt = jax.nn.sigmoid(_bdot(u, wg_ref[...]))
    yc = _bdot(oc_ref[...], wao_ref[...])
    merged = (gt[:, :D_MODEL] * ya_ref[...] + gt[:, D_MODEL:2 * D_MODEL] * yb_ref[...]
              + gt[:, 2 * D_MODEL:] * yc)
    o_ref[...] = x + _bdot(merged.astype(BF16), wo_ref[...])


def _merge(x, l, gain, w_gates, ya, yb, oc, w_attn_out, w_o):
    n = x.shape[0]
    tm = MERGE_TM
    tile = pl.BlockSpec((tm, D_MODEL), lambda i: (i, 0))
    return pl.pallas_call(
        _merge_body,
        out_shape=jax.ShapeDtypeStruct((n, D_MODEL), F32),
        grid=(n // tm,),
        in_specs=[tile, _layer_spec(l, (1, D_MODEL)), _layer_spec(l, (D_MODEL, N_BRANCHES * D_MODEL)),
                  tile, tile, tile,
                  _layer_spec(l, (N_HEADS * HEAD_DIM, D_MODEL)), _layer_spec(l, (D_MODEL, D_MODEL))],
        out_specs=tile,
        compiler_params=_cparams(("arbitrary",)),
        name="merge",
    )(x, gain, w_gates, ya, yb, oc, w_attn_out, w_o)


def _block_diag_groups(w):
    depth = w.shape[0]
    per = GATE_GROUP // RNN_BLOCK_W
    w5 = w.reshape(depth, N_GATE_GROUPS, per, RNN_BLOCK_W, RNN_BLOCK_W)
    eye = jnp.eye(per, dtype=w.dtype)
    return jnp.einsum("lgnij,nm->lgnimj", w5, eye).reshape(depth, N_GATE_GROUPS, GATE_GROUP, GATE_GROUP)


def kernel(x, positions, ffn1_norm, ffn1_w_gate_up, ffn1_w_down, mix_norm, w_in, rnn_conv_w, rnn_conv_b,
           rnn_gate_a_w, rnn_gate_a_b, rnn_gate_x_w, rnn_gate_x_b, rnn_lambda, rnn_w_out, sconv_w,
           sconv_w_out, attn_w_out, w_o, ffn2_norm, ffn2_w_gate_up, ffn2_w_down, final_norm):
    batch, seq, d = x.shape
    depth = w_in.shape[0]
    n = batch * seq
    xf = x.reshape(n, d)
    rows = lambda v: v.reshape(depth, 1, -1)
    bf = lambda w: w.astype(BF16)

    tabs = _rope_tables(positions.reshape(n, 1))

    o_v = 2 * D_RNN + 3 * D_SCONV + (N_HEADS + N_KV_HEADS) * HEAD_DIM
    o_qi = o_v + N_KV_HEADS * HEAD_DIM
    o_wi = o_qi + N_IDX_HEADS * IDX_DIM + IDX_DIM
    o_gates = o_wi + N_IDX_HEADS

    w_in_b = bf(w_in)
    w_gates = bf(w_in[:, :, o_gates:])
    w_t = bf(jnp.swapaxes(jnp.concatenate([w_in[:, :, o_v:o_qi], w_in[:, :, o_wi:o_gates]], axis=2), 1, 2))
    wbd = bf(jnp.concatenate([_block_diag_groups(rnn_gate_a_w), _block_diag_groups(rnn_gate_x_w)], axis=-1))
    ffn1_gu, ffn1_d, ffn2_gu, ffn2_d = bf(ffn1_w_gate_up), bf(ffn1_w_down), bf(ffn2_w_gate_up), bf(ffn2_w_down)
    rnn_wo, sconv_wo, attn_wo, w_o_b = bf(rnn_w_out), bf(sconv_w_out), bf(attn_w_out), bf(w_o)
    g_ffn1, g_mix, g_ffn2 = rows(ffn1_norm), rows(mix_norm), rows(ffn2_norm)
    conv_b, gate_a_b, gate_x_b, lam = rows(rnn_conv_b), rows(rnn_gate_a_b), rows(rnn_gate_x_b), rows(rnn_lambda)
    g_final = final_norm.reshape(1, -1)

    for l in range(depth):
        xf = _ffn(xf, l, g_ffn1, ffn1_gu, ffn1_d, g_final, False)
        ya = _rnn_branch(xf, l, g_mix, w_in_b, rnn_conv_w, conv_b, wbd, gate_a_b, gate_x_b, lam, rnn_wo, seq)
        yb = _sconv_branch(xf, l, g_mix, w_in_b, sconv_w, sconv_wo, seq)
        q, k, qi, ki, vt, wit = _qkv_proj(xf, l, g_mix, w_in_b, w_t, tabs)
        oc = _attention(q, qi, wit, ki, k, vt, batch, seq)
        xf = _merge(xf, l, g_mix, w_gates, ya, yb, oc, attn_wo, w_o_b)
        xf = _ffn(xf, l, g_ffn2, ffn2_gu, ffn2_d, g_final, l == depth - 1)

    return xf.reshape(batch, seq, d)
```
